```python
import jax, jax.numpy as jnp
from jax import lax
import numpy as np

D_MODEL = 1024
BATCH = 8
SEQ = 4096
DEPTH = 1

CONV_WIDTH = D_MODEL
CONV_KERNEL = 31
HEAD_DIM = 128
HEADS_PER_GROUP = D_MODEL // 256
DILATED_GROUPS = ((128, 1), (512, 4), (2048, 16))
N_GROUPS = len(DILATED_GROUPS)
ATTN_HEADS = HEADS_PER_GROUP * N_GROUPS
QKV_WIDTH = ATTN_HEADS * HEAD_DIM
ATTN_WIDTH = HEADS_PER_GROUP * HEAD_DIM
BLK = 128
ROPE_THETA = 10000.0
EPS = 1e-6
NEG_INF = -1e30

SPLIT_SIZES = (CONV_WIDTH, CONV_WIDTH, CONV_WIDTH,
               QKV_WIDTH, QKV_WIDTH, QKV_WIDTH,
               ATTN_WIDTH,
               D_MODEL, D_MODEL)
D_IN = sum(SPLIT_SIZES)
SPLIT_POINTS = tuple(int(v) for v in np.cumsum(SPLIT_SIZES)[:-1])

kernel_name = "hybrid_conformer_dilated_attn_block"


def rms_norm(x, w):
    xf = x.astype(jnp.float32)
    return xf * lax.rsqrt(jnp.mean(xf * xf, axis=-1, keepdims=True) + EPS) * w.astype(jnp.float32)


def layer_norm(x, w, b):
    xf = x.astype(jnp.float32)
    mu = jnp.mean(xf, axis=-1, keepdims=True)
    var = jnp.mean(jnp.square(xf - mu), axis=-1, keepdims=True)
    return (xf - mu) * lax.rsqrt(var + EPS) * w.astype(jnp.float32) + b.astype(jnp.float32)


def rope(x, positions):
    inv_freq = ROPE_THETA ** (-jnp.arange(0, HEAD_DIM, 2, dtype=jnp.float32) / HEAD_DIM)
    ang = positions.astype(jnp.float32)[..., None] * inv_freq
    cos = jnp.cos(ang)[:, :, None, :]
    sin = jnp.sin(ang)[:, :, None, :]
    xf = x.astype(jnp.float32)
    x1, x2 = xf[..., : HEAD_DIM // 2], xf[..., HEAD_DIM // 2:]
    return jnp.concatenate([x1 * cos - x2 * sin, x2 * cos + x1 * sin], axis=-1).astype(x.dtype)


def depthwise_causal_conv(u, w, b):
    out = lax.conv_general_dilated(
        u, w.astype(u.dtype)[:, None, :], window_strides=(1,),
        padding=[(CONV_KERNEL - 1, 0)], dimension_numbers=("NWC", "WIO", "NWC"),
        feature_group_count=u.shape[-1])
    return out + b.astype(u.dtype)


def dilated_window_attention(q, k, v, window, dilation):
    bsz, seq, nh, hd = q.shape
    steps = window // dilation
    span = dilation * BLK
    seq_pad = -(-seq // span) * span
    nb = seq_pad // span
    pad = ((0, 0), (0, seq_pad - seq), (0, 0), (0, 0))

    def blocks(t):
        return jnp.pad(t, pad).reshape(bsz, nb, BLK, dilation, nh, hd)

    def with_prev(t):
        prev = jnp.pad(t[:, :-1], ((0, 0), (1, 0), (0, 0), (0, 0), (0, 0), (0, 0)))
        return jnp.concatenate([prev, t], axis=2)

    qb = blocks(q)
    kk = with_prev(blocks(k))
    vv = with_prev(blocks(v))
    s = jnp.einsum("bnqrhd,bnkrhd->bnrhqk", qb, kk).astype(jnp.float32) * (hd ** -0.5)
    qi = jnp.arange(BLK)[:, None]
    kj = jnp.arange(2 * BLK)[None, :]
    diff = BLK + qi - kj
    band = (diff >= 0) & (diff <= steps)
    valid_prev = (jnp.arange(nb)[:, None, None] > 0) | (kj[None] >= BLK)
    mask = band[None] & valid_prev
    s = jnp.where(mask[:, None, None], s, NEG_INF)
    lse = jax.nn.logsumexp(s, axis=-1)
    p = jnp.exp(s - lse[..., None])
    o = jnp.einsum("bnrhqk,bnkrhd->bnqrhd", p.astype(v.dtype), vv)
    o = o.reshape(bsz, seq_pad, nh, hd)[:, :seq]
    lse = lse.transpose(0, 1, 4, 2, 3).reshape(bsz, seq_pad, nh)[:, :seq]
    return o, lse


def setup_inputs(seed: int = 0) -> dict:
    key = jax.random.key(seed)
    ks = jax.random.split(key, 17)
    f32 = jnp.float32
    nrm = lambda k, shape, s: jax.random.normal(k, shape, f32) * s
    return {
        "x": nrm(ks[0], (BATCH, SEQ, D_MODEL), 1.0),
        "c": nrm(ks[1], (BATCH, D_MODEL), 1.0),
        "positions": jnp.broadcast_to(jnp.arange(SEQ, dtype=jnp.int32), (BATCH, SEQ)),
        "norm_w": 1.0 + nrm(ks[2], (D_MODEL,), 0.02),
        "w_ada": nrm(ks[3], (D_MODEL, 3 * D_MODEL), 0.5 * D_MODEL ** -0.5),
        "b_ada": nrm(ks[4], (3 * D_MODEL,), 0.02),
        "w_in": nrm(ks[5], (D_MODEL, D_IN), D_MODEL ** -0.5),
        "conv_w": nrm(ks[6], (CONV_KERNEL, CONV_WIDTH), CONV_KERNEL ** -0.5),
        "conv_b": nrm(ks[7], (CONV_WIDTH,), 0.02),
        "conv_ln_w": 1.0 + nrm(ks[8], (CONV_WIDTH,), 0.02),
        "conv_ln_b": nrm(ks[9], (CONV_WIDTH,), 0.02),
        "w_conv_out": nrm(ks[10], (CONV_WIDTH, D_MODEL), CONV_WIDTH ** -0.5),
        "q_norm_w": 1.0 + nrm(ks[11], (HEAD_DIM,), 0.02),
        "k_norm_w": 1.0 + nrm(ks[12], (HEAD_DIM,), 0.02),
        "w_attn_out": nrm(ks[13], (ATTN_WIDTH, D_MODEL), ATTN_WIDTH ** -0.5),
        "w_out": nrm(ks[14], (D_MODEL, D_MODEL), D_MODEL ** -0.5),
    }


def reference(x, c, positions, norm_w, w_ada, b_ada, w_in, conv_w, conv_b, conv_ln_w,
              conv_ln_b, w_conv_out, q_norm_w, k_norm_w, w_attn_out, w_out):
    dt = x.dtype
    bsz, seq, _ = x.shape
    for _layer in range(DEPTH):
        mod = jax.nn.silu(c) @ w_ada + b_ada
        shift, scale, gate = jnp.split(mod, 3, axis=-1)
        h = (rms_norm(x, norm_w) * (1.0 + scale[:, None].astype(jnp.float32))
             + shift[:, None].astype(jnp.float32)).astype(dt)

        z = h @ w_in
        a, b, g_conv, q, k, v, g_attn, m_conv, m_attn = jnp.split(z, SPLIT_POINTS, axis=-1)

        u = a * jax.nn.sigmoid(b)
        u = depthwise_causal_conv(u, conv_w, conv_b)
        u = jax.nn.silu(layer_norm(u, conv_ln_w, conv_ln_b)).astype(dt)
        u = u * jax.nn.silu(g_conv)
        y_conv = u @ w_conv_out

        q = q.reshape(bsz, seq, ATTN_HEADS, HEAD_DIM)
        k = k.reshape(bsz, seq, ATTN_HEADS, HEAD_DIM)
        v = v.reshape(bsz, seq, ATTN_HEADS, HEAD_DIM)
        q = rope(rms_norm(q, q_norm_w).astype(dt), positions)
        k = rope(rms_norm(k, k_norm_w).astype(dt), positions)
        outs, lses = [], []
        for g, (window, dilation) in enumerate(DILATED_GROUPS):
            sl = slice(g * HEADS_PER_GROUP, (g + 1) * HEADS_PER_GROUP)
            o_g, l_g = dilated_window_attention(q[:, :, sl], k[:, :, sl], v[:, :, sl], window, dilation)
            outs.append(o_g)
            lses.append(l_g)
        o = jnp.stack(outs, axis=0)
        wts = jax.nn.softmax(jnp.stack(lses, axis=0), axis=0)
        o = jnp.sum(wts[..., None] * o.astype(jnp.float32), axis=0).astype(dt)
        o = o.reshape(bsz, seq, ATTN_WIDTH) * jax.nn.silu(g_attn)
        y_attn = o @ w_attn_out

        y = jax.nn.sigmoid(m_conv) * y_conv + jax.nn.sigmoid(m_attn) * y_attn
        out = y @ w_out
        x = (x + gate[:, None] * out).astype(dt)
    return x
```

```python
import functools

import jax
import jax.numpy as jnp
import numpy as np
from jax import lax
from jax.experimental import pallas as pl
from jax.experimental.pallas import tpu as pltpu

D_MODEL = 1024
CONV_KERNEL = 31
HEAD_DIM = 128
HEADS_PER_GROUP = 4
GROUP_WIDTH = HEADS_PER_GROUP * HEAD_DIM
DILATED_GROUPS = ((128, 1), (512, 4), (2048, 16))
N_GROUPS = len(DILATED_GROUPS)
QKV_WIDTH = N_GROUPS * GROUP_WIDTH
BLK = 128
ROPE_THETA = 10000.0
EPS = 1e-6
NEG_INF = -1e30
BIG = 3.0e38

OFF_A, OFF_B, OFF_GC = 0, 1024, 2048
OFF_Q, OFF_K, OFF_V = 3072, 4608, 6144
OFF_GA, OFF_MC, OFF_MA = 7680, 8192, 9216

HALO = 32
VMEM_LIMIT = 56 * 1024 * 1024

TM_QKV = 512
TM_TAIL = 256
ATTN_UNITS = 8


def _params(n_axes):
    return pltpu.CompilerParams(dimension_semantics=("arbitrary",) * n_axes,
                                vmem_limit_bytes=VMEM_LIMIT)


def _const_spec(shape):
    nd = len(shape)
    return pl.BlockSpec(shape, lambda *_: (0,) * nd, pipeline_mode=pl.Buffered(1))


def _sigmoid(v):
    return jax.nn.sigmoid(v)


def _silu(v):
    return v * jax.nn.sigmoid(v)


def _bdot(a, b):
    return jnp.dot(a, b, preferred_element_type=jnp.float32)


def _mod_kernel(c_ref, w_ref, b_ref, o_ref):
    sc = _silu(c_ref[...])
    o_ref[...] = jnp.dot(sc, w_ref[...], preferred_element_type=jnp.float32,
                         precision=lax.Precision.HIGHEST) + b_ref[...]


def _mod_call(c, w_ada, b_ada):
    bsz = c.shape[0]
    n_out = w_ada.shape[1]
    bn = 1024
    return pl.pallas_call(
        _mod_kernel,
        grid=(n_out // bn,),
        in_specs=[pl.BlockSpec((bsz, D_MODEL), lambda j: (0, 0)),
                  pl.BlockSpec((D_MODEL, bn), lambda j: (0, j)),
                  pl.BlockSpec((1, bn), lambda j: (0, j))],
        out_specs=pl.BlockSpec((bsz, bn), lambda j: (0, j)),
        out_shape=jax.ShapeDtypeStruct((bsz, n_out), jnp.float32),
        compiler_params=_params(1),
        name="mod",
    )(c, w_ada, b_ada.reshape(1, n_out))


def _normed_input(x, norm_w, scale, shift):
    ms = jnp.mean(x * x, axis=-1, keepdims=True)
    return x * lax.rsqrt(ms + EPS) * norm_w * (1.0 + scale) + shift


def _qkv_kernel(x_ref, shift_ref, scale_ref, nw_ref, w_ref, qnw_ref, knw_ref, pos_ref, freq_ref,
                *out_refs):
    x = x_ref[0]
    h = _normed_input(x, nw_ref[...], scale_ref[0], shift_ref[0]).astype(jnp.bfloat16)
    ang = pos_ref[0] * freq_ref[...]
    cos_t = jnp.cos(ang)
    lane = lax.broadcasted_iota(jnp.int32, ang.shape, 1)
    sin_t = jnp.where(lane < HEAD_DIM // 2, -1.0, 1.0) * jnp.sin(ang)
    q_scale = HEAD_DIM ** -0.5
    for idx in range(3 * N_GROUPS):
        kind = idx // N_GROUPS
        z = _bdot(h, w_ref[:, idx * GROUP_WIDTH:(idx + 1) * GROUP_WIDTH])
        o_ref = out_refs[idx]
        if kind == 2:
            o_ref[0] = z.astype(o_ref.dtype)
            continue
        nw = (qnw_ref if kind == 0 else knw_ref)[...]
        for j in range(HEADS_PER_GROUP):
            zh = z[:, j * HEAD_DIM:(j + 1) * HEAD_DIM]
            ms = jnp.mean(zh * zh, axis=-1, keepdims=True)
            zn = zh * lax.rsqrt(ms + EPS) * nw
            r = zn * cos_t + pltpu.roll(zn, HEAD_DIM // 2, axis=1) * sin_t
            if kind == 0:
                r = r * q_scale
            o_ref[0, :, j * HEAD_DIM:(j + 1) * HEAD_DIM] = r.astype(o_ref.dtype)


def _qkv_call(x, mod3, norm_w, w_qkv, q_norm_w, k_norm_w, posf, freq2):
    bsz, seq, _ = x.shape
    tm = TM_QKV
    tok = lambda b, s: (b, s, 0)
    out_sds = jax.ShapeDtypeStruct((bsz, seq, GROUP_WIDTH), jnp.bfloat16)
    return pl.pallas_call(
        _qkv_kernel,
        grid=(bsz, seq // tm),
        in_specs=[pl.BlockSpec((1, tm, D_MODEL), tok),
                  pl.BlockSpec((1, 1, D_MODEL), lambda b, s: (3 * b, 0, 0)),
                  pl.BlockSpec((1, 1, D_MODEL), lambda b, s: (3 * b + 1, 0, 0)),
                  _const_spec((1, D_MODEL)),
                  _const_spec(w_qkv.shape),
                  _const_spec((1, HEAD_DIM)),
                  _const_spec((1, HEAD_DIM)),
                  pl.BlockSpec((1, tm, 1), tok),
                  _const_spec((1, HEAD_DIM))],
        out_specs=[pl.BlockSpec((1, tm, GROUP_WIDTH), tok)] * (3 * N_GROUPS),
        out_shape=[out_sds] * (3 * N_GROUPS),
        compiler_params=_params(2),
        name="qkv",
    )(x, mod3, mod3, norm_w.reshape(1, D_MODEL), w_qkv, q_norm_w.reshape(1, HEAD_DIM),
      k_norm_w.reshape(1, HEAD_DIM), posf, freq2)


def _attn_kernel(*refs, nb_step, res_step, first_group, last_group):
    if first_group:
        q_ref, k_ref, v_ref, kp_ref, vp_ref, cap_ref = refs[:6]
        rest = refs[6:]
        oa_ref = la_ref = None
    else:
        q_ref, k_ref, v_ref, kp_ref, vp_ref, cap_ref, oa_ref, la_ref = refs[:8]
        rest = refs[8:]
    if last_group:
        (o_ref,) = rest
        l_ref = None
    else:
        o_ref, l_ref = rest

    no_prev = (pl.program_id(1) == 0).astype(jnp.int32)
    cap_cur = cap_ref[2]
    for i in range(nb_step):
        cap_prev = cap_ref[no_prev] if i == 0 else cap_ref[0]
        for r in range(res_step):
            for j in range(HEADS_PER_GROUP):
                c0 = r * GROUP_WIDTH + j * HEAD_DIM
                cols = slice(c0, c0 + HEAD_DIM)
                q = q_ref[0, i, :, cols]
                kc = k_ref[0, i, :, cols]
                vc = v_ref[0, i, :, cols]
                if i == 0:
                    kp = kp_ref[0, 0, :, cols]
                    vp = vp_ref[0, 0, :, cols]
                else:
                    kp = k_ref[0, i - 1, :, cols]
                    vp = v_ref[0, i - 1, :, cols]
                dn = (((1,), (1,)), ((), ()))
                s_p = lax.dot_general(q, kp, dn, preferred_element_type=jnp.float32)
                s_c = lax.dot_general(q, kc, dn, preferred_element_type=jnp.float32)
                s_p = jnp.minimum(s_p, cap_prev)
                s_c = jnp.minimum(s_c, cap_cur)
                m = jnp.maximum(jnp.max(s_p, axis=-1, keepdims=True),
                                jnp.max(s_c, axis=-1, keepdims=True))
                p_p = jnp.exp(s_p - m)
                p_c = jnp.exp(s_c - m)
                den = jnp.sum(p_p, axis=-1, keepdims=True) + jnp.sum(p_c, axis=-1, keepdims=True)
                o_un = _bdot(p_p.astype(vp.dtype), vp) + _bdot(p_c.astype(vc.dtype), vc)
                o_g = o_un * (1.0 / den)
                lse = jnp.broadcast_to(m + jnp.log(den), o_g.shape)
                if first_group:
                    o_new, l_new = o_g, lse
                else:
                    la = la_ref[0, i, :, cols]
                    oa = oa_ref[0, i, :, cols]
                    mx = jnp.maximum(la, lse)
                    ea = jnp.exp(la - mx)
                    eg = jnp.exp(lse - mx)
                    tot = ea + eg
                    inv = 1.0 / tot
                    o_new = oa * (ea * inv) + o_g * (eg * inv)
                    l_new = mx + jnp.log(tot)
                o_ref[0, i, :, cols] = o_new
                if l_ref is not None:
                    l_ref[0, i, :, cols] = l_new


def _attn_call(q, k, v, caps, o_acc, l_acc, dilation, last_group):
    bsz, seq, _ = q.shape
    first_group = o_acc is None
    span = dilation * BLK
    nb = seq // span
    res_step = min(dilation, ATTN_UNITS)
    nb_step = min(ATTN_UNITS // res_step, nb)
    width = dilation * GROUP_WIDTH
    bw = res_step * GROUP_WIDTH
    view = lambda t: t.reshape(bsz, nb, BLK, width)

    cur = pl.BlockSpec((1, nb_step, BLK, bw), lambda b, n, rc: (b, n, 0, rc))
    prev = pl.BlockSpec((1, 1, BLK, bw),
                        lambda b, n, rc: (b, jnp.maximum(n * nb_step - 1, 0), 0, rc))
    in_specs = [cur, cur, cur, prev, prev, _const_spec(caps.shape)]
    args = [view(q), view(k), view(v), view(k), view(v), caps]
    if not first_group:
        in_specs += [cur, cur]
        args += [view(o_acc), view(l_acc)]
    acc_sds = jax.ShapeDtypeStruct((bsz, nb, BLK, width), jnp.float32)
    out_specs, out_shape = [cur], [acc_sds]
    if not last_group:
        out_specs, out_shape = [cur, cur], [acc_sds, acc_sds]
    outs = pl.pallas_call(
        functools.partial(_attn_kernel, nb_step=nb_step, res_step=res_step,
                          first_group=first_group, last_group=last_group),
        grid=(bsz, nb // nb_step, dilation // res_step),
        in_specs=in_specs,
        out_specs=out_specs,
        out_shape=out_shape,
        compiler_params=_params(3),
        name=f"attn_d{dilation}",
    )(*args)
    return [t.reshape(bsz, seq, GROUP_WIDTH) for t in outs]


def _tail_kernel(x_ref, shift_ref, scale_ref, gate_ref, nw_ref, o_ref,
                 wa_ref, wb_ref, wgc_ref, wga_ref, wmc_ref, wma_ref,
                 cw_ref, cb_ref, lnw_ref, lnb_ref, wco_ref, wao_ref, wout_ref,
                 out_ref, ubuf_ref):
    tm = x_ref.shape[1]

    @pl.when(pl.program_id(1) == 0)
    def _():
        ubuf_ref[0:HALO, :] = jnp.zeros((HALO, D_MODEL), jnp.float32)

    x = x_ref[0]
    h = _normed_input(x, nw_ref[...], scale_ref[0], shift_ref[0]).astype(jnp.bfloat16)

    u = _bdot(h, wa_ref[...]) * _sigmoid(_bdot(h, wb_ref[...]))
    ubuf_ref[HALO:HALO + tm, :] = u
    acc = jnp.broadcast_to(cb_ref[...], (tm, D_MODEL))
    for t in range(CONV_KERNEL):
        start = HALO - (CONV_KERNEL - 1) + t
        acc = acc + cw_ref[t:t + 1, :] * ubuf_ref[start:start + tm, :]
    ubuf_ref[0:HALO, :] = ubuf_ref[tm:tm + HALO, :]
    mu = jnp.mean(acc, axis=-1, keepdims=True)
    cen = acc - mu
    var = jnp.mean(cen * cen, axis=-1, keepdims=True)
    ln = cen * lax.rsqrt(var + EPS) * lnw_ref[...] + lnb_ref[...]
    tc = (_silu(ln) * _silu(_bdot(h, wgc_ref[...]))).astype(jnp.bfloat16)
    y_conv = _bdot(tc, wco_ref[...])

    ta = (o_ref[0] * _silu(_bdot(h, wga_ref[...]))).astype(jnp.bfloat16)
    y_attn = _bdot(ta, wao_ref[...])

    y = _sigmoid(_bdot(h, wmc_ref[...])) * y_conv + _sigmoid(_bdot(h, wma_ref[...])) * y_attn
    out = _bdot(y.astype(jnp.bfloat16), wout_ref[...])
    out_ref[0] = x + gate_ref[0] * out


def _tail_call(x, mod3, norm_w, o_attn, weights, conv_w, conv_b, ln_w, ln_b, w_co, w_ao, w_out):
    bsz, seq, _ = x.shape
    tm = TM_TAIL
    tok = lambda b, s: (b, s, 0)
    row = lambda v: v.reshape(1, D_MODEL)
    modspec = lambda k: pl.BlockSpec((1, 1, D_MODEL), lambda b, s: (3 * b + k, 0, 0))
    in_specs = ([pl.BlockSpec((1, tm, D_MODEL), tok), modspec(0), modspec(1), modspec(2),
                 _const_spec((1, D_MODEL)),
                 pl.BlockSpec((1, tm, GROUP_WIDTH), tok)]
                + [_const_spec(w.shape) for w in weights]
                + [_const_spec(conv_w.shape)] + [_const_spec((1, D_MODEL))] * 3
                + [_const_spec(w_co.shape), _const_spec(w_ao.shape), _const_spec(w_out.shape)])
    return pl.pallas_call(
        _tail_kernel,
        grid=(bsz, seq // tm),
        in_specs=in_specs,
        out_specs=pl.BlockSpec((1, tm, D_MODEL), tok),
        out_shape=jax.ShapeDtypeStruct(x.shape, x.dtype),
        scratch_shapes=[pltpu.VMEM((tm + HALO, D_MODEL), jnp.float32)],
        compiler_params=_params(2),
        name="tail",
    )(x, mod3, mod3, mod3, row(norm_w), o_attn, *weights, conv_w, row(conv_b), row(ln_w),
      row(ln_b), w_co, w_ao, w_out)


def _attention_caps():
    qi = np.arange(BLK)[:, None]
    kj = np.arange(BLK)[None, :]
    allow_prev = kj >= qi
    allow_cur = kj <= qi
    caps = np.stack([np.where(allow_prev, BIG, NEG_INF),
                     np.full((BLK, BLK), NEG_INF),
                     np.where(allow_cur, BIG, NEG_INF)]).astype(np.float32)
    return jnp.asarray(caps)


def kernel(x, c, positions, norm_w, w_ada, b_ada, w_in, conv_w, conv_b, conv_ln_w, conv_ln_b,
           w_conv_out, q_norm_w, k_norm_w, w_attn_out, w_out):
    bsz, seq, _ = x.shape
    bf = jnp.bfloat16
    for window, dilation in DILATED_GROUPS:
        assert window // dilation == BLK and seq % (dilation * BLK) == 0
    assert seq % TM_QKV == 0 and seq % TM_TAIL == 0

    mod3 = _mod_call(c, w_ada, b_ada).reshape(bsz * 3, 1, D_MODEL)

    w_qkv = w_in[:, OFF_Q:OFF_GA].astype(bf)
    inv_freq = ROPE_THETA ** (-jnp.arange(0, HEAD_DIM, 2, dtype=jnp.float32) / HEAD_DIM)
    freq2 = jnp.concatenate([inv_freq, inv_freq]).reshape(1, HEAD_DIM)
    posf = positions.astype(jnp.float32)[..., None]
    qkv = _qkv_call(x, mod3, norm_w, w_qkv, q_norm_w, k_norm_w, posf, freq2)

    caps = _attention_caps()
    o_acc = l_acc = None
    for g, (_, dilation) in enumerate(DILATED_GROUPS):
        last = g == N_GROUPS - 1
        outs = _attn_call(qkv[g], qkv[N_GROUPS + g], qkv[2 * N_GROUPS + g], caps, o_acc, l_acc,
                          dilation, last)
        o_acc = outs[0]
        l_acc = None if last else outs[1]

    cols = lambda off, n: w_in[:, off:off + n].astype(bf)
    weights = [cols(OFF_A, D_MODEL), cols(OFF_B, D_MODEL), cols(OFF_GC, D_MODEL),
               cols(OFF_GA, GROUP_WIDTH), cols(OFF_MC, D_MODEL), cols(OFF_MA, D_MODEL)]
    return _tail_call(x, mod3, norm_w, o_acc, weights, conv_w, conv_b, conv_ln_w, conv_ln_b,
                      w_conv_out.astype(bf), w_attn_out.astype(bf), w_out.astype(bf))
```

```python
import functools

import jax
import jax.numpy as jnp
import numpy as np
from jax import lax
from jax.experimental import pallas as pl
from jax.experimental.pallas import tpu as pltpu

D_MODEL = 1024
CONV_KERNEL = 31
HEAD_DIM = 128
HEADS_PER_GROUP = 4
GROUP_WIDTH = HEADS_PER_GROUP * HEAD_DIM
DILATED_GROUPS = ((128, 1), (512, 4), (2048, 16))
N_GROUPS = len(DILATED_GROUPS)
QKV_WIDTH = N_GROUPS * GROUP_WIDTH
BLK = 128
ROPE_THETA = 10000.0
EPS = 1e-6
NEG_INF = -1e30
BIG = 3.0e38

OFF_A, OFF_B, OFF_GC = 0, 1024, 2048
OFF_Q, OFF_K, OFF_V = 3072, 4608, 6144
OFF_GA, OFF_MC, OFF_MA = 7680, 8192, 9216

LANES = 128
HALO = 32
CONV_CHUNK = 16
VMEM_LIMIT = 56 * 1024 * 1024

TM_QKV = 512
TM_TAIL = 512
ATTN_UNITS = 8


def _params(n_axes):
    return pltpu.CompilerParams(dimension_semantics=("arbitrary",) * n_axes,
                                vmem_limit_bytes=VMEM_LIMIT)


def _const_spec(shape):
    nd = len(shape)
    return pl.BlockSpec(shape, lambda *_: (0,) * nd, pipeline_mode=pl.Buffered(1))


def _sigmoid(v):
    return jax.nn.sigmoid(v)


def _silu(v):
    return v * jax.nn.sigmoid(v)


def _bdot(a, b):
    return jnp.dot(a, b, preferred_element_type=jnp.float32)


def _mod_kernel(c_ref, w_ref, b_ref, o_ref):
    sc = _silu(c_ref[...])
    o_ref[...] = jnp.dot(sc, w_ref[...], preferred_element_type=jnp.float32,
                         precision=lax.Precision.HIGHEST) + b_ref[...]


def _mod_call(c, w_ada, b_ada):
    bsz = c.shape[0]
    n_out = w_ada.shape[1]
    bn = 1024
    return pl.pallas_call(
        _mod_kernel,
        grid=(n_out // bn,),
        in_specs=[pl.BlockSpec((bsz, D_MODEL), lambda j: (0, 0)),
                  pl.BlockSpec((D_MODEL, bn), lambda j: (0, j)),
                  pl.BlockSpec((1, bn), lambda j: (0, j))],
        out_specs=pl.BlockSpec((bsz, bn), lambda j: (0, j)),
        out_shape=jax.ShapeDtypeStruct((bsz, n_out), jnp.float32),
        compiler_params=_params(1),
        name="mod",
    )(c, w_ada, b_ada.reshape(1, n_out))


def _normed_input(x, norm_w, scale, shift):
    ms = jnp.mean(x * x, axis=-1, keepdims=True)
    return x * lax.rsqrt(ms + EPS) * norm_w * (1.0 + scale) + shift


def _qkv_kernel(x_ref, shift_ref, scale_ref, nw_ref, w_ref, qnw_ref, knw_ref, pos_ref, freq_ref,
                *out_refs):
    x = x_ref[0]
    h = _normed_input(x, nw_ref[...], scale_ref[0], shift_ref[0]).astype(jnp.bfloat16)
    half_rows = x.shape[0] // 2
    half = HEAD_DIM // 2
    pos = pos_ref[0]
    lane = lax.broadcasted_iota(jnp.int32, (half_rows, HEAD_DIM), 1)
    low = lane < half
    ang = jnp.where(low, pos[:half_rows], pos[half_rows:]) * freq_ref[...]
    cos_p, sin_p = jnp.cos(ang), jnp.sin(ang)
    cos_s, sin_s = pltpu.roll(cos_p, half, axis=1), pltpu.roll(sin_p, half, axis=1)
    cos_t = jnp.concatenate([jnp.where(low, cos_p, cos_s), jnp.where(low, cos_s, cos_p)], axis=0)
    sin_t = jnp.concatenate([jnp.where(low, -sin_p, sin_s), jnp.where(low, -sin_s, sin_p)], axis=0)
    q_scale = HEAD_DIM ** -0.5
    for idx in range(3 * N_GROUPS):
        kind = idx // N_GROUPS
        z = _bdot(h, w_ref[:, idx * GROUP_WIDTH:(idx + 1) * GROUP_WIDTH])
        o_ref = out_refs[idx]
        if kind == 2:
            o_ref[0] = z.astype(o_ref.dtype)
            continue
        nw = (qnw_ref if kind == 0 else knw_ref)[...]
        for j in range(HEADS_PER_GROUP):
            zh = z[:, j * HEAD_DIM:(j + 1) * HEAD_DIM]
            ms = jnp.mean(zh * zh, axis=-1, keepdims=True)
            zn = zh * lax.rsqrt(ms + EPS) * nw
            r = zn * cos_t + pltpu.roll(zn, HEAD_DIM // 2, axis=1) * sin_t
            if kind == 0:
                r = r * q_scale
            o_ref[0, :, j * HEAD_DIM:(j + 1) * HEAD_DIM] = r.astype(o_ref.dtype)


def _qkv_call(x, mod3, norm_w, w_qkv, q_norm_w, k_norm_w, posf, freq2):
    bsz, seq, _ = x.shape
    tm = TM_QKV
    tok = lambda b, s: (b, s, 0)
    out_sds = jax.ShapeDtypeStruct((bsz, seq, GROUP_WIDTH), jnp.bfloat16)
    return pl.pallas_call(
        _qkv_kernel,
        grid=(bsz, seq // tm),
        in_specs=[pl.BlockSpec((1, tm, D_MODEL), tok),
                  pl.BlockSpec((1, 1, D_MODEL), lambda b, s: (3 * b, 0, 0)),
                  pl.BlockSpec((1, 1, D_MODEL), lambda b, s: (3 * b + 1, 0, 0)),
                  _const_spec((1, D_MODEL)),
                  _const_spec(w_qkv.shape),
                  _const_spec((1, HEAD_DIM)),
                  _const_spec((1, HEAD_DIM)),
                  pl.BlockSpec((1, tm, 1), tok),
                  _const_spec((1, HEAD_DIM))],
        out_specs=[pl.BlockSpec((1, tm, GROUP_WIDTH), tok)] * (3 * N_GROUPS),
        out_shape=[out_sds] * (3 * N_GROUPS),
        compiler_params=_params(2),
        name="qkv",
    )(x, mod3, mod3, norm_w.reshape(1, D_MODEL), w_qkv, q_norm_w.reshape(1, HEAD_DIM),
      k_norm_w.reshape(1, HEAD_DIM), posf, freq2)


def _attn_kernel(*refs, nb_step, res_step, first_group, last_group):
    if first_group:
        q_ref, k_ref, v_ref, kp_ref, vp_ref, cap_ref = refs[:6]
        rest = refs[6:]
        oa_ref = la_ref = None
    else:
        q_ref, k_ref, v_ref, kp_ref, vp_ref, cap_ref, oa_ref, la_ref = refs[:8]
        rest = refs[8:]
    if last_group:
        (o_ref,) = rest
        l_ref = None
    else:
        o_ref, l_ref = rest

    no_prev = (pl.program_id(1) == 0).astype(jnp.int32)
    cap_cur = cap_ref[2]
    for i in range(nb_step):
        cap_prev = cap_ref[no_prev] if i == 0 else cap_ref[0]
        for r in range(res_step):
            for j in range(HEADS_PER_GROUP):
                c0 = r * GROUP_WIDTH + j * HEAD_DIM
                cols = slice(c0, c0 + HEAD_DIM)
                q = q_ref[0, i, :, cols]
                kc = k_ref[0, i, :, cols]
                vc = v_ref[0, i, :, cols]
                if i == 0:
                    kp = kp_ref[0, 0, :, cols]
                    vp = vp_ref[0, 0, :, cols]
                else:
                    kp = k_ref[0, i - 1, :, cols]
                    vp = v_ref[0, i - 1, :, cols]
                dn = (((1,), (1,)), ((), ()))
                s_p = lax.dot_general(q, kp, dn, preferred_element_type=jnp.float32)
                s_c = lax.dot_general(q, kc, dn, preferred_element_type=jnp.float32)
                s_p = jnp.minimum(s_p, cap_prev)
                s_c = jnp.minimum(s_c, cap_cur)
                m = jnp.maximum(jnp.max(s_p, axis=-1, keepdims=True),
                                jnp.max(s_c, axis=-1, keepdims=True))
                p_p = jnp.exp(s_p - m)
                p_c = jnp.exp(s_c - m)
                den = jnp.sum(p_p, axis=-1, keepdims=True) + jnp.sum(p_c, axis=-1, keepdims=True)
                o_un = _bdot(p_p.astype(vp.dtype), vp) + _bdot(p_c.astype(vc.dtype), vc)
                o_g = o_un * (1.0 / den)
                lse = jnp.broadcast_to(m + jnp.log(den), o_g.shape)
                if first_group:
                    o_new, l_new = o_g, lse
                else:
                    la = la_ref[0, i, :, cols]
                    oa = oa_ref[0, i, :, cols]
                    mx = jnp.maximum(la, lse)
                    ea = jnp.exp(la - mx)
                    eg = jnp.exp(lse - mx)
                    tot = ea + eg
                    inv = 1.0 / tot
                    o_new = oa * (ea * inv) + o_g * (eg * inv)
                    l_new = mx + jnp.log(tot)
                o_ref[0, i, :, cols] = o_new
                if l_ref is not None:
                    l_ref[0, i, :, cols] = l_new


def _attn_call(q, k, v, caps, o_acc, l_acc, dilation, last_group):
    bsz, seq, _ = q.shape
    first_group = o_acc is None
    span = dilation * BLK
    nb = seq // span
    res_step = min(dilation, ATTN_UNITS)
    nb_step = min(ATTN_UNITS // res_step, nb)
    width = dilation * GROUP_WIDTH
    bw = res_step * GROUP_WIDTH
    view = lambda t: t.reshape(bsz, nb, BLK, width)

    cur = pl.BlockSpec((1, nb_step, BLK, bw), lambda b, n, rc: (b, n, 0, rc))
    prev = pl.BlockSpec((1, 1, BLK, bw),
                        lambda b, n, rc: (b, jnp.maximum(n * nb_step - 1, 0), 0, rc))
    in_specs = [cur, cur, cur, prev, prev, _const_spec(caps.shape)]
    args = [view(q), view(k), view(v), view(k), view(v), caps]
    if not first_group:
        in_specs += [cur, cur]
        args += [view(o_acc), view(l_acc)]
    acc_sds = jax.ShapeDtypeStruct((bsz, nb, BLK, width), jnp.float32)
    out_specs, out_shape = [cur], [acc_sds]
    if not last_group:
        out_specs, out_shape = [cur, cur], [acc_sds, acc_sds]
    outs = pl.pallas_call(
        functools.partial(_attn_kernel, nb_step=nb_step, res_step=res_step,
                          first_group=first_group, last_group=last_group),
        grid=(bsz, nb // nb_step, dilation // res_step),
        in_specs=in_specs,
        out_specs=out_specs,
        out_shape=out_shape,
        compiler_params=_params(3),
        name=f"attn_d{dilation}",
    )(*args)
    return [t.reshape(bsz, seq, GROUP_WIDTH) for t in outs]


def _tail_kernel(x_ref, shift_ref, scale_ref, gate_ref, nw_ref, o_ref,
                 wa_ref, wb_ref, wgc_ref, wga_ref, wmc_ref, wma_ref,
                 cw_ref, cb_ref, lnw_ref, lnb_ref, wco_ref, wao_ref, wout_ref,
                 out_ref, useq_ref, cseq_ref):
    tm = x_ref.shape[1]
    nsl = D_MODEL // LANES

    @pl.when(pl.program_id(1) == 0)
    def _():
        useq_ref[0:HALO * nsl, :] = jnp.zeros((HALO * nsl, LANES), jnp.float32)

    x = x_ref[0]
    h = _normed_input(x, nw_ref[...], scale_ref[0], shift_ref[0]).astype(jnp.bfloat16)

    u = _bdot(h, wa_ref[...]) * _sigmoid(_bdot(h, wb_ref[...]))
    for s in range(nsl):
        useq_ref[pl.ds(HALO * nsl + s, tm, stride=nsl), :] = u[:, s * LANES:(s + 1) * LANES]

    for c in range(tm // CONV_CHUNK):
        base = c * CONV_CHUNK * nsl
        acc = jnp.broadcast_to(cb_ref[...][None], (CONV_CHUNK, nsl, LANES))
        for t in range(CONV_KERNEL):
            start = base + (HALO - (CONV_KERNEL - 1) + t) * nsl
            taps = useq_ref[pl.ds(start, CONV_CHUNK * nsl), :].reshape(CONV_CHUNK, nsl, LANES)
            acc = acc + cw_ref[t][None] * taps
        cseq_ref[pl.ds(base, CONV_CHUNK * nsl), :] = acc.reshape(CONV_CHUNK * nsl, LANES)
    useq_ref[0:HALO * nsl, :] = useq_ref[tm * nsl:(tm + HALO) * nsl, :]
    acc = jnp.concatenate([cseq_ref[pl.ds(s, tm, stride=nsl), :] for s in range(nsl)], axis=-1)
    mu = jnp.mean(acc, axis=-1, keepdims=True)
    cen = acc - mu
    var = jnp.mean(cen * cen, axis=-1, keepdims=True)
    ln = cen * lax.rsqrt(var + EPS) * lnw_ref[...] + lnb_ref[...]
    tc = (_silu(ln) * _silu(_bdot(h, wgc_ref[...]))).astype(jnp.bfloat16)
    y_conv = _bdot(tc, wco_ref[...])

    ta = (o_ref[0] * _silu(_bdot(h, wga_ref[...]))).astype(jnp.bfloat16)
    y_attn = _bdot(ta, wao_ref[...])

    y = _sigmoid(_bdot(h, wmc_ref[...])) * y_conv + _sigmoid(_bdot(h, wma_ref[...])) * y_attn
    out = _bdot(y.astype(jnp.bfloat16), wout_ref[...])
    out_ref[0] = x + gate_ref[0] * out


def _tail_call(x, mod3, norm_w, o_attn, weights, conv_w, conv_b, ln_w, ln_b, w_co, w_ao, w_out):
    bsz, seq, _ = x.shape
    tm = TM_TAIL
    nsl = D_MODEL // LANES
    tok = lambda b, s: (b, s, 0)
    row = lambda v: v.reshape(1, D_MODEL)
    modspec = lambda k: pl.BlockSpec((1, 1, D_MODEL), lambda b, s: (3 * b + k, 0, 0))
    in_specs = ([pl.BlockSpec((1, tm, D_MODEL), tok), modspec(0), modspec(1), modspec(2),
                 _const_spec((1, D_MODEL)),
                 pl.BlockSpec((1, tm, GROUP_WIDTH), tok)]
                + [_const_spec(w.shape) for w in weights]
                + [_const_spec((CONV_KERNEL, nsl, LANES)), _const_spec((nsl, LANES))]
                + [_const_spec((1, D_MODEL))] * 2
                + [_const_spec(w_co.shape), _const_spec(w_ao.shape), _const_spec(w_out.shape)])
    return pl.pallas_call(
        _tail_kernel,
        grid=(bsz, seq // tm),
        in_specs=in_specs,
        out_specs=pl.BlockSpec((1, tm, D_MODEL), tok),
        out_shape=jax.ShapeDtypeStruct(x.shape, x.dtype),
        scratch_shapes=[pltpu.VMEM(((tm + HALO) * nsl, LANES), jnp.float32),
                        pltpu.VMEM((tm * nsl, LANES), jnp.float32)],
        compiler_params=_params(2),
        name="tail",
    )(x, mod3, mod3, mod3, row(norm_w), o_attn, *weights, conv_w.reshape(CONV_KERNEL, nsl, LANES),
      conv_b.reshape(nsl, LANES), row(ln_w), row(ln_b), w_co, w_ao, w_out)


def _attention_caps():
    qi = np.arange(BLK)[:, None]
    kj = np.arange(BLK)[None, :]
    allow_prev = kj >= qi
    allow_cur = kj <= qi
    caps = np.stack([np.where(allow_prev, BIG, NEG_INF),
                     np.full((BLK, BLK), NEG_INF),
                     np.where(allow_cur, BIG, NEG_INF)]).astype(np.float32)
    return jnp.asarray(caps)


def kernel(x, c, positions, norm_w, w_ada, b_ada, w_in, conv_w, conv_b, conv_ln_w, conv_ln_b,
           w_conv_out, q_norm_w, k_norm_w, w_attn_out, w_out):
    bsz, seq, _ = x.shape
    bf = jnp.bfloat16
    for window, dilation in DILATED_GROUPS:
        assert window // dilation == BLK and seq % (dilation * BLK) == 0
    assert seq % TM_QKV == 0 and seq % TM_TAIL == 0

    mod3 = _mod_call(c, w_ada, b_ada).reshape(bsz * 3, 1, D_MODEL)

    w_qkv = w_in[:, OFF_Q:OFF_GA].astype(bf)
    inv_freq = ROPE_THETA ** (-jnp.arange(0, HEAD_DIM, 2, dtype=jnp.float32) / HEAD_DIM)
    freq2 = jnp.concatenate([inv_freq, inv_freq]).reshape(1, HEAD_DIM)
    posf = positions.astype(jnp.float32)[..., None]
    qkv = _qkv_call(x, mod3, norm_w, w_qkv, q_norm_w, k_norm_w, posf, freq2)

    caps = _attention_caps()
    o_acc = l_acc = None
    for g, (_, dilation) in enumerate(DILATED_GROUPS):
        last = g == N_GROUPS - 1
        outs = _attn_call(qkv[g], qkv[N_GROUPS + g], qkv[2 * N_GROUPS + g], caps, o_acc, l_acc,
                          dilation, last)
        o_acc = outs[0]
        l_acc = None if last else outs[1]

    cols = lambda off, n: w_in[:, off:off + n].astype(bf)
    weights = [cols(OFF_A, D_MODEL), cols(OFF_B, D_MODEL), cols(OFF_GC, D_MODEL),
               cols(OFF_GA, GROUP_WIDTH), cols(OFF_MC, D_MODEL), cols(OFF_MA, D_MODEL)]
    return _tail_call(x, mod3, norm_w, o_acc, weights, conv_w, conv_b, conv_ln_w, conv_ln_b,
                      w_conv_out.astype(bf), w_attn_out.astype(bf), w_out.astype(bf))
```

```python
import functools
import math

import jax
import jax.numpy as jnp
import numpy as np
from jax import lax
from jax.experimental import pallas as pl
from jax.experimental.pallas import tpu as pltpu

D_MODEL = 1024
CONV_KERNEL = 31
HEAD_DIM = 128
HEADS_PER_GROUP = 4
GROUP_WIDTH = HEADS_PER_GROUP * HEAD_DIM
DILATED_GROUPS = ((128, 1), (512, 4), (2048, 16))
N_GROUPS = len(DILATED_GROUPS)
BLK = 128
ROPE_THETA = 10000.0
EPS = 1e-6
NEG_INF = -1e30
BIG = 3.0e38
LOG2E = math.log2(math.e)

OFF_A, OFF_B, OFF_GC = 0, 1024, 2048
OFF_Q, OFF_K, OFF_V = 3072, 4608, 6144
OFF_GA, OFF_MC, OFF_MA = 7680, 8192, 9216

LANES = 128
SUB_ROWS = 16
CHUNKS_PER_BLK = BLK // SUB_ROWS
PERM_ROWS = 256
HALO = 32
CONV_CHUNK = 16
VMEM_LIMIT = 56 * 1024 * 1024

TM_QKV = 512
TM_TAIL = 512
ATTN_PAIRS = 16
PAIRS_PER_BODY = 4


def _params(n_axes):
    return pltpu.CompilerParams(dimension_semantics=("arbitrary",) * n_axes,
                                vmem_limit_bytes=VMEM_LIMIT)


def _const_spec(shape):
    nd = len(shape)
    return pl.BlockSpec(shape, lambda *_: (0,) * nd, pipeline_mode=pl.Buffered(1))


def _sigmoid(v):
    return jax.nn.sigmoid(v)


def _silu(v):
    return v * jax.nn.sigmoid(v)


def _bdot(a, b):
    return jnp.dot(a, b, preferred_element_type=jnp.float32)


def _residue_perm(dilation):
    chunk = SUB_ROWS * dilation
    p = np.zeros((PERM_ROWS, PERM_ROWS), np.float32)
    for a in range(PERM_ROWS // chunk):
        for r in range(dilation):
            for j in range(SUB_ROWS):
                p[a * chunk + r * SUB_ROWS + j, a * chunk + j * dilation + r] = 1.0
    return p


def _mod_kernel(c_ref, w_ref, b_ref, o_ref):
    sc = _silu(c_ref[...])
    o_ref[...] = jnp.dot(sc, w_ref[...], preferred_element_type=jnp.float32,
                         precision=lax.Precision.HIGHEST) + b_ref[...]


def _mod_call(c, w_ada, b_ada):
    bsz = c.shape[0]
    n_out = w_ada.shape[1]
    bn = 1024
    return pl.pallas_call(
        _mod_kernel,
        grid=(n_out // bn,),
        in_specs=[pl.BlockSpec((bsz, D_MODEL), lambda j: (0, 0)),
                  pl.BlockSpec((D_MODEL, bn), lambda j: (0, j)),
                  pl.BlockSpec((1, bn), lambda j: (0, j))],
        out_specs=pl.BlockSpec((bsz, bn), lambda j: (0, j)),
        out_shape=jax.ShapeDtypeStruct((bsz, n_out), jnp.float32),
        compiler_params=_params(1),
        name="mod",
    )(c, w_ada, b_ada.reshape(1, n_out))


def _normed_input(x, norm_w, scale, shift):
    ms = jnp.mean(x * x, axis=-1, keepdims=True)
    return x * lax.rsqrt(ms + EPS) * norm_w * (1.0 + scale) + shift


def _grouped_shape(bsz, seq, dilation, width):
    return (bsz, seq // (SUB_ROWS * dilation), dilation, SUB_ROWS, width)


def _grouped_tile_spec(tm, dilation, width):
    return pl.BlockSpec((1, tm // (SUB_ROWS * dilation), dilation, SUB_ROWS, width),
                        lambda b, s: (b, s, 0, 0, 0))


def _qkv_kernel(x_ref, shift_ref, scale_ref, nw_ref, w_ref, qnw_ref, knw_ref, pos_ref, freq_ref,
                perm_ref, *out_refs):
    x = x_ref[0]
    tm = x.shape[0]
    h = _normed_input(x, nw_ref[...], scale_ref[0], shift_ref[0]).astype(jnp.bfloat16)
    half_rows = tm // 2
    half = HEAD_DIM // 2
    pos = pos_ref[0]
    lane = lax.broadcasted_iota(jnp.int32, (half_rows, HEAD_DIM), 1)
    low = lane < half
    ang = jnp.where(low, pos[:half_rows], pos[half_rows:]) * freq_ref[...]
    cos_p, sin_p = jnp.cos(ang), jnp.sin(ang)
    cos_s, sin_s = pltpu.roll(cos_p, half, axis=1), pltpu.roll(sin_p, half, axis=1)
    cos_t = jnp.concatenate([jnp.where(low, cos_p, cos_s), jnp.where(low, cos_s, cos_p)], axis=0)
    sin_t = jnp.concatenate([jnp.where(low, -sin_p, sin_s), jnp.where(low, -sin_s, sin_p)], axis=0)
    norm_ws = (qnw_ref[...] * (LOG2E * HEAD_DIM ** -0.5), knw_ref[...])
    for idx in range(3 * N_GROUPS):
        kind, g = divmod(idx, N_GROUPS)
        z = _bdot(h, w_ref[:, idx * GROUP_WIDTH:(idx + 1) * GROUP_WIDTH])
        if kind < 2:
            heads = []
            for j in range(HEADS_PER_GROUP):
                zh = z[:, j * HEAD_DIM:(j + 1) * HEAD_DIM]
                ms = jnp.mean(zh * zh, axis=-1, keepdims=True)
                zn = zh * lax.rsqrt(ms + EPS) * norm_ws[kind]
                heads.append(zn * cos_t + pltpu.roll(zn, half, axis=1) * sin_t)
            z = jnp.concatenate(heads, axis=-1)
        zb = z.astype(jnp.bfloat16)
        if g > 0:
            zb = jnp.concatenate(
                [_bdot(perm_ref[g - 1], zb[lo:lo + PERM_ROWS]).astype(jnp.bfloat16)
                 for lo in range(0, tm, PERM_ROWS)], axis=0)
        o_ref = out_refs[idx]
        o_ref[0] = zb.reshape(o_ref.shape[1:])


def _qkv_call(x, mod3, norm_w, w_qkv, q_norm_w, k_norm_w, posf, freq2, perms):
    bsz, seq, _ = x.shape
    tm = TM_QKV
    tok = lambda b, s: (b, s, 0)
    dils = [d for _, d in DILATED_GROUPS]
    return pl.pallas_call(
        _qkv_kernel,
        grid=(bsz, seq // tm),
        in_specs=[pl.BlockSpec((1, tm, D_MODEL), tok),
                  pl.BlockSpec((1, 1, D_MODEL), lambda b, s: (3 * b, 0, 0)),
                  pl.BlockSpec((1, 1, D_MODEL), lambda b, s: (3 * b + 1, 0, 0)),
                  _const_spec((1, D_MODEL)),
                  _const_spec(w_qkv.shape),
                  _const_spec((1, HEAD_DIM)),
                  _const_spec((1, HEAD_DIM)),
                  pl.BlockSpec((1, tm, 1), tok),
                  _const_spec((1, HEAD_DIM)),
                  _const_spec(perms.shape)],
        out_specs=[_grouped_tile_spec(tm, d, GROUP_WIDTH) for d in dils] * 3,
        out_shape=[jax.ShapeDtypeStruct(_grouped_shape(bsz, seq, d, GROUP_WIDTH), jnp.bfloat16)
                   for d in dils] * 3,
        compiler_params=_params(2),
        name="qkv",
    )(x, mod3, mod3, norm_w.reshape(1, D_MODEL), w_qkv, q_norm_w.reshape(1, HEAD_DIM),
      k_norm_w.reshape(1, HEAD_DIM), posf, freq2, perms)


def _attn_kernel(q_ref, k_ref, v_ref, kp_ref, vp_ref, cap_ref, o_ref, l_ref, *, nb_step, res_step):
    no_prev = (pl.program_id(1) == 0).astype(jnp.int32)
    ones =jnp.ones((BLK, HEAD_DIM), jnp.bfloat16)
    lane = lax.broadcasted_iota(jnp.int32, (BLK, HEAD_DIM), 1)
    nt_dims = (((1,), (1,)), ((), ()))

    def tile(ref, chunk0, r, j):
        t = ref[0, pl.ds(chunk0, CHUNKS_PER_BLK), r, :, j * HEAD_DIM:(j + 1) * HEAD_DIM]
        return t.reshape(BLK, HEAD_DIM)

    def process(pairs, first):
        units = [(i, r, j) for i, r in pairs for j in range(HEADS_PER_GROUP)]
        cap = cap_ref[no_prev] if first else cap_ref[0]
        q = [tile(q_ref, i * CHUNKS_PER_BLK, r, j) for i, r, j in units]
        kc = [tile(k_ref, i * CHUNKS_PER_BLK, r, j) for i, r, j in units]
        vc = [tile(v_ref, i * CHUNKS_PER_BLK, r, j) for i, r, j in units]
        if first:
            kp = [tile(kp_ref, 0, r, j) for _, r, j in units]
            vp = [tile(vp_ref, 0, r, j) for _, r, j in units]
        else:
            kp = [tile(k_ref, (i - 1) * CHUNKS_PER_BLK, r, j) for i, r, j in units]
            vp = [tile(v_ref, (i - 1) * CHUNKS_PER_BLK, r, j) for i, r, j in units]
        s = [lax.dot_general(a, jnp.concatenate([b, c], axis=0), nt_dims, preferred_element_type=jnp.float32)
             for a, b, c in zip(q, kp, kc)]
        s = [jnp.minimum(t, cap) for t in s]
        m = [jnp.max(t, axis=-1, keepdims=True) for t in s]
        p = [jnp.exp2(t - mm).astype(jnp.bfloat16) for t, mm in zip(s, m)]
        oe = [_bdot(t, jnp.concatenate([jnp.concatenate([va, ones], axis=-1),
                                        jnp.concatenate([vb, ones], axis=-1)], axis=0))
              for t, va, vb in zip(p, vp, vc)]
        for n, (i, r) in enumerate(pairs):
            chunks = pl.ds(i * CHUNKS_PER_BLK, CHUNKS_PER_BLK)
            stats = jnp.zeros((BLK, HEAD_DIM), jnp.float32)
            for j in range(HEADS_PER_GROUP):
                u = n * HEADS_PER_GROUP + j
                o_ref[0, chunks, r, :, j * HEAD_DIM:(j + 1) * HEAD_DIM] = (
                    oe[u][:, :HEAD_DIM].astype(o_ref.dtype).reshape(CHUNKS_PER_BLK, SUB_ROWS, HEAD_DIM))
                stats = jnp.where(lane == j, m[u], stats)
                stats = jnp.where(lane == HEADS_PER_GROUP + j, oe[u][:, HEAD_DIM:], stats)
            l_ref[0, chunks, r, :, :] = stats.reshape(CHUNKS_PER_BLK, SUB_ROWS, HEAD_DIM)

    def run(n_pairs, pair_of, first):
        left = n_pairs % PAIRS_PER_BODY
        if left:
            process([pair_of(t) for t in range(left)], first)
        if n_pairs >= PAIRS_PER_BODY:
            def body(t, carry):
                t0 = left + PAIRS_PER_BODY * t
                process([pair_of(t0 + n) for n in range(PAIRS_PER_BODY)], first)
                return carry
            lax.fori_loop(0, n_pairs // PAIRS_PER_BODY, body, 0)

    shift = res_step.bit_length() - 1

    def later_pair(t):
        return 1 + (t >> shift), t & (res_step - 1)

    run(res_step, lambda t: (0, t), True)
    run((nb_step - 1) * res_step, later_pair, False)


def _attn_call(q, k, v, caps, dilation):
    bsz, n_chunks = q.shape[:2]
    nb = n_chunks // CHUNKS_PER_BLK
    res_step = min(dilation, ATTN_PAIRS)
    nb_step = min(ATTN_PAIRS // res_step, nb)
    assert res_step & (res_step - 1) == 0 and nb % nb_step == 0 and dilation % res_step == 0

    def cur(width):
        return pl.BlockSpec((1, nb_step * CHUNKS_PER_BLK, res_step, SUB_ROWS, width),
                            lambda b, n, rc: (b, n, rc, 0, 0))
    prev = pl.BlockSpec((1, CHUNKS_PER_BLK, res_step, SUB_ROWS, GROUP_WIDTH),
                        lambda b, n, rc: (b, jnp.maximum(n * nb_step - 1, 0), rc, 0, 0))
    qkv_spec = cur(GROUP_WIDTH)
    return pl.pallas_call(
        functools.partial(_attn_kernel, nb_step=nb_step, res_step=res_step),
        grid=(bsz, nb // nb_step, dilation // res_step),
        in_specs=[qkv_spec, qkv_spec, qkv_spec, prev, prev, _const_spec(caps.shape)],
        out_specs=[qkv_spec, cur(HEAD_DIM)],
        out_shape=[jax.ShapeDtypeStruct(q.shape, jnp.bfloat16),
                   jax.ShapeDtypeStruct(q.shape[:-1] + (HEAD_DIM,), jnp.float32)],
        compiler_params=_params(3),
        name=f"attn_d{dilation}",
    )(q, k, v, k, v, caps)


def _tail_kernel(x_ref, shift_ref, scale_ref, gate_ref, nw_ref,
                 o0_ref, o1_ref, o2_ref, l0_ref, l1_ref, l2_ref, pinv_ref, pinv32_ref,
                 wa_ref, wb_ref, wgc_ref, wga_ref, wmc_ref, wma_ref,
                 cw_ref, cb_ref, lnw_ref, lnb_ref, wco_ref, wao_ref, wout_ref,
                 out_ref, useq_ref, cseq_ref):
    tm = x_ref.shape[1]
    nsl = D_MODEL // LANES

    @pl.when(pl.program_id(1) == 0)
    def _():
        useq_ref[0:HALO * nsl, :] = jnp.zeros((HALO * nsl, LANES), jnp.float32)

    x = x_ref[0]
    h = _normed_input(x, nw_ref[...], scale_ref[0], shift_ref[0]).astype(jnp.bfloat16)

    u = _bdot(h, wa_ref[...]) * _sigmoid(_bdot(h, wb_ref[...]))
    for s in range(nsl):
        useq_ref[pl.ds(HALO * nsl + s, tm, stride=nsl), :] = u[:, s * LANES:(s + 1) * LANES]
    for c in range(tm // CONV_CHUNK):
        base = c * CONV_CHUNK * nsl
        acc = jnp.broadcast_to(cb_ref[...][None], (CONV_CHUNK, nsl, LANES))
        for t in range(CONV_KERNEL):
            start = base + (HALO - (CONV_KERNEL - 1) + t) * nsl
            taps = useq_ref[pl.ds(start, CONV_CHUNK * nsl), :].reshape(CONV_CHUNK, nsl, LANES)
            acc = acc + cw_ref[t][None] * taps
        cseq_ref[pl.ds(base, CONV_CHUNK * nsl), :] = acc.reshape(CONV_CHUNK * nsl, LANES)
    useq_ref[0:HALO * nsl, :] = useq_ref[tm * nsl:(tm + HALO) * nsl, :]
    acc = jnp.concatenate([cseq_ref[pl.ds(s, tm, stride=nsl), :] for s in range(nsl)], axis=-1)
    mu = jnp.mean(acc, axis=-1, keepdims=True)
    cen = acc - mu
    var = jnp.mean(cen * cen, axis=-1, keepdims=True)
    ln = cen * lax.rsqrt(var + EPS) * lnw_ref[...] + lnb_ref[...]
    tc = (_silu(ln) * _silu(_bdot(h, wgc_ref[...]))).astype(jnp.bfloat16)
    y_conv = _bdot(tc, wco_ref[...])

    def token_order(ref, g, pmat_ref, **kw):
        rows = ref[0].reshape(tm, ref.shape[-1])
        if g == 0:
            return rows.astype(jnp.float32)
        return jnp.concatenate(
            [jnp.dot(pmat_ref[g - 1], rows[lo:lo + PERM_ROWS], preferred_element_type=jnp.float32, **kw)
             for lo in range(0, tm, PERM_ROWS)], axis=0)

    o_g = [token_order(r, g, pinv_ref) for g, r in enumerate((o0_ref, o1_ref, o2_ref))]
    stats = [token_order(r, g, pinv32_ref, precision=lax.Precision.HIGHEST)
             for g, r in enumerate((l0_ref, l1_ref, l2_ref))]
    top = jnp.maximum(jnp.maximum(stats[0], stats[1]), stats[2])
    e_g = [jnp.exp2(st - top) for st in stats]
    dens = [pltpu.roll(st, LANES - HEADS_PER_GROUP, axis=1) for st in stats]
    total = e_g[0] * dens[0] + e_g[1] * dens[1] + e_g[2] * dens[2]
    head_lane = lax.broadcasted_iota(jnp.int32, total.shape, 1) < HEADS_PER_GROUP
    inv = 1.0 / jnp.where(head_lane, total, 1.0)
    w_g = [e * inv for e in e_g]
    heads = []
    for j in range(HEADS_PER_GROUP):
        cols = slice(j * HEAD_DIM, (j + 1) * HEAD_DIM)
        heads.append(sum(w[:, j:j + 1] * o[:, cols] for w, o in zip(w_g, o_g)))
    o = jnp.concatenate(heads, axis=-1)
    ta = (o * _silu(_bdot(h, wga_ref[...]))).astype(jnp.bfloat16)
    y_attn = _bdot(ta, wao_ref[...])

    y = _sigmoid(_bdot(h, wmc_ref[...])) * y_conv + _sigmoid(_bdot(h, wma_ref[...])) * y_attn
    out = _bdot(y.astype(jnp.bfloat16), wout_ref[...])
    out_ref[0] = x + gate_ref[0] * out


def _tail_call(x, mod3, norm_w, o_groups, l_groups, pinv, pinv32, weights, conv_w, conv_b, ln_w, ln_b,
               w_co, w_ao, w_out):
    bsz, seq, _ = x.shape
    tm = TM_TAIL
    nsl = D_MODEL // LANES
    tok = lambda b, s: (b, s, 0)
    row = lambda v: v.reshape(1, D_MODEL)
    modspec = lambda k: pl.BlockSpec((1, 1, D_MODEL), lambda b, s: (3 * b + k, 0, 0))
    dils = [d for _, d in DILATED_GROUPS]
    in_specs = ([pl.BlockSpec((1, tm, D_MODEL), tok), modspec(0), modspec(1), modspec(2),
                 _const_spec((1, D_MODEL))]
                + [_grouped_tile_spec(tm, d, GROUP_WIDTH) for d in dils]
                + [_grouped_tile_spec(tm, d, HEAD_DIM) for d in dils]
                + [_const_spec(pinv.shape), _const_spec(pinv32.shape)]
                + [_const_spec(w.shape) for w in weights]
                + [_const_spec((CONV_KERNEL, nsl, LANES)), _const_spec((nsl, LANES))]
                + [_const_spec((1, D_MODEL))] * 2
                + [_const_spec(w_co.shape), _const_spec(w_ao.shape), _const_spec(w_out.shape)])
    return pl.pallas_call(
        _tail_kernel,
        grid=(bsz, seq // tm),
        in_specs=in_specs,
        out_specs=pl.BlockSpec((1, tm, D_MODEL), tok),
        out_shape=jax.ShapeDtypeStruct(x.shape, x.dtype),
        scratch_shapes=[pltpu.VMEM(((tm + HALO) * nsl, LANES), jnp.float32),
                        pltpu.VMEM((tm * nsl, LANES), jnp.float32)],
        compiler_params=_params(2),
        name="tail",
    )(x, mod3, mod3, mod3, row(norm_w), *o_groups, *l_groups, pinv, pinv32, *weights,
      conv_w.reshape(CONV_KERNEL, nsl, LANES), conv_b.reshape(nsl, LANES), row(ln_w), row(ln_b),
      w_co, w_ao, w_out)


def _attention_caps():
    qi = np.arange(BLK)[:, None]
    kj = np.arange(BLK)[None, :]
    prev = np.where(kj >= qi, BIG, NEG_INF)
    cur = np.where(kj <= qi, BIG, NEG_INF)
    no_prev = np.full((BLK, BLK), NEG_INF)
    caps = np.stack([np.concatenate([prev, cur], axis=1),
                     np.concatenate([no_prev, cur], axis=1)]).astype(np.float32)
    return jnp.asarray(caps)


def kernel(x, c, positions, norm_w, w_ada, b_ada, w_in, conv_w, conv_b, conv_ln_w, conv_ln_b,
           w_conv_out, q_norm_w, k_norm_w, w_attn_out, w_out):
    bsz, seq, _ = x.shape
    bf = jnp.bfloat16
    for window, dilation in DILATED_GROUPS:
        assert window // dilation == BLK and seq % (dilation * BLK) == 0
        assert PERM_ROWS % (SUB_ROWS * dilation) == 0
    assert seq % TM_QKV == 0 and seq % TM_TAIL == 0
    assert TM_QKV % PERM_ROWS == 0 and TM_TAIL % PERM_ROWS == 0

    mod3 = _mod_call(c, w_ada, b_ada).reshape(bsz * 3, 1, D_MODEL)

    w_qkv = w_in[:, OFF_Q:OFF_GA].astype(bf)
    inv_freq = ROPE_THETA ** (-jnp.arange(0, HEAD_DIM, 2, dtype=jnp.float32) / HEAD_DIM)
    freq2 = jnp.concatenate([inv_freq, inv_freq]).reshape(1, HEAD_DIM)
    posf = positions.astype(jnp.float32)[..., None]
    perm_np = np.stack([_residue_perm(d) for _, d in DILATED_GROUPS[1:]])
    perms = jnp.asarray(perm_np, bf)
    pinv32 = jnp.asarray(perm_np.transpose(0, 2, 1))
    qkv = _qkv_call(x, mod3, norm_w, w_qkv, q_norm_w, k_norm_w, posf, freq2, perms)

    caps = _attention_caps()
    o_groups, l_groups = [], []
    for g, (_, dilation) in enumerate(DILATED_GROUPS):
        o_g, l_g = _attn_call(qkv[g], qkv[N_GROUPS + g], qkv[2 * N_GROUPS + g], caps, dilation)
        o_groups.append(o_g)
        l_groups.append(l_g)

    cols = lambda off, n: w_in[:, off:off + n].astype(bf)
    weights = [cols(OFF_A, D_MODEL), cols(OFF_B, D_MODEL), cols(OFF_GC, D_MODEL),
               cols(OFF_GA, GROUP_WIDTH), cols(OFF_MC, D_MODEL), cols(OFF_MA, D_MODEL)]
    return _tail_call(x, mod3, norm_w, o_groups, l_groups, pinv32.astype(bf), pinv32, weights,
                      conv_w, conv_b, conv_ln_w, conv_ln_b,
                      w_conv_out.astype(bf), w_attn_out.astype(bf), w_out.astype(bf))
```

```python
import functools
import math

import jax
import jax.numpy as jnp
import numpy as np
from jax import lax
from jax.experimental import pallas as pl
from jax.experimental.pallas import tpu as pltpu

D_MODEL = 1024
CONV_KERNEL = 31
HEAD_DIM = 128
HEADS_PER_GROUP = 4
GROUP_WIDTH = HEADS_PER_GROUP * HEAD_DIM
DILATED_GROUPS = ((128, 1), (512, 4), (2048, 16))
N_GROUPS = len(DILATED_GROUPS)
BLK = 128
ROPE_THETA = 10000.0
EPS = 1e-6
NEG_INF = -1e30
BIG = 3.0e38
LOG2E = math.log2(math.e)

OFF_A, OFF_B, OFF_GC = 0, 1024, 2048
OFF_Q, OFF_K, OFF_V = 3072, 4608, 6144
OFF_GA, OFF_MC, OFF_MA = 7680, 8192, 9216

LANES = 128
SUB_ROWS = 16
CHUNKS_PER_BLK = BLK // SUB_ROWS
PERM_ROWS = 256
HALO = 32
CONV_CHUNK = 16
SIDE_PIECE = 256
SIDE_LAG = 8
VMEM_LIMIT = 56 * 1024 * 1024

TM_QKV = 512
TM_TAIL = 512
ATTN_PAIRS = 16
PAIRS_PER_BODY = 4


def _params(n_axes):
    return pltpu.CompilerParams(dimension_semantics=("arbitrary",) * n_axes,
                                vmem_limit_bytes=VMEM_LIMIT)


def _const_spec(shape):
    nd = len(shape)
    return pl.BlockSpec(shape, lambda *_: (0,) * nd, pipeline_mode=pl.Buffered(1))


def _sigmoid(v):
    return jax.nn.sigmoid(v)


def _silu(v):
    return v * jax.nn.sigmoid(v)


def _bdot(a, b):
    return jnp.dot(a, b, preferred_element_type=jnp.float32)


def _zero_from(v):
    bits = pltpu.bitcast(v, jnp.uint32)
    return pltpu.bitcast((bits >> 16) >> 16, jnp.float32)


def _residue_perm(dilation):
    chunk = SUB_ROWS * dilation
    p = np.zeros((PERM_ROWS, PERM_ROWS), np.float32)
    for a in range(PERM_ROWS // chunk):
        for r in range(dilation):
            for j in range(SUB_ROWS):
                p[a * chunk + r * SUB_ROWS + j, a * chunk + j * dilation + r] = 1.0
    return p


def _mod_kernel(c_ref, w_ref, b_ref, o_ref):
    sc = _silu(c_ref[...])
    o_ref[...] = jnp.dot(sc, w_ref[...], preferred_element_type=jnp.float32,
                         precision=lax.Precision.HIGHEST) + b_ref[...]


def _mod_call(c, w_ada, b_ada):
    bsz = c.shape[0]
    n_out = w_ada.shape[1]
    bn = 1024
    return pl.pallas_call(
        _mod_kernel,
        grid=(n_out // bn,),
        in_specs=[pl.BlockSpec((bsz, D_MODEL), lambda j: (0, 0)),
                  pl.BlockSpec((D_MODEL, bn), lambda j: (0, j)),
                  pl.BlockSpec((1, bn), lambda j: (0, j))],
        out_specs=pl.BlockSpec((bsz, bn), lambda j: (0, j)),
        out_shape=jax.ShapeDtypeStruct((bsz, n_out), jnp.float32),
        compiler_params=_params(1),
        name="mod",
    )(c, w_ada, b_ada.reshape(1, n_out))


def _normed_input(x, norm_w, scale, shift):
    ms = jnp.mean(x * x, axis=-1, keepdims=True)
    return x * lax.rsqrt(ms + EPS) * norm_w * (1.0 + scale) + shift


def _grouped_shape(bsz, seq, dilation, width):
    return (bsz, seq // (SUB_ROWS * dilation), dilation, SUB_ROWS, width)


def _grouped_tile_spec(tm, dilation, width):
    return pl.BlockSpec((1, tm // (SUB_ROWS * dilation), dilation, SUB_ROWS, width),
                        lambda b, s: (b, s, 0, 0, 0))


def _qkv_kernel(x_ref, shift_ref, scale_ref, nw_ref, wq_ref, wk_ref, wv_ref, qnw_ref, knw_ref, pos_ref,
                freq_ref, perm_ref, *out_and_scratch):
    *out_refs, z_ref = out_and_scratch
    w_refs = (wq_ref, wk_ref, wv_ref)
    x = x_ref[0]
    tm = x.shape[0]
    h = _normed_input(x, nw_ref[...], scale_ref[0], shift_ref[0]).astype(jnp.bfloat16)
    half_rows = tm // 2
    half = HEAD_DIM // 2
    pos = pos_ref[0]
    lane = lax.broadcasted_iota(jnp.int32, (half_rows, HEAD_DIM), 1)
    low = lane < half
    ang = jnp.where(low, pos[:half_rows], pos[half_rows:]) * freq_ref[...]
    cos_p, sin_p = jnp.cos(ang), jnp.sin(ang)
    cos_s, sin_s = pltpu.roll(cos_p, half, axis=1), pltpu.roll(sin_p, half, axis=1)
    cos_t = jnp.concatenate([jnp.where(low, cos_p, cos_s), jnp.where(low, cos_s, cos_p)], axis=0)
    sin_t = jnp.concatenate([jnp.where(low, -sin_p, sin_s), jnp.where(low, -sin_s, sin_p)], axis=0)
    norm_ws = (qnw_ref[...] * (LOG2E * HEAD_DIM ** -0.5), knw_ref[...])
    n_proj = 3 * N_GROUPS
    for idx in range(n_proj):
        kind, g = divmod(idx, N_GROUPS)
        z_ref[idx] = _bdot(h, w_refs[kind][:, g * GROUP_WIDTH:(g + 1) * GROUP_WIDTH])
    for idx in range(n_proj):
        kind, g = divmod(idx, N_GROUPS)
        z = z_ref[idx]
        if kind < 2:
            heads = []
            for j in range(HEADS_PER_GROUP):
                zh = z[:, j * HEAD_DIM:(j + 1) * HEAD_DIM]
                ms = jnp.mean(zh * zh, axis=-1, keepdims=True)
                zn = zh * lax.rsqrt(ms + EPS) * norm_ws[kind]
                heads.append(zn * cos_t + pltpu.roll(zn, half, axis=1) * sin_t)
            z = jnp.concatenate(heads, axis=-1)
        zb = z.astype(jnp.bfloat16)
        if g > 0:
            zb = jnp.concatenate(
                [_bdot(perm_ref[g - 1], zb[lo:lo + PERM_ROWS]).astype(jnp.bfloat16)
                 for lo in range(0, tm, PERM_ROWS)], axis=0)
        o_ref = out_refs[idx]
        o_ref[0] = zb.reshape(o_ref.shape[1:])


def _weight_cols_spec(offset, width):
    assert offset % width == 0
    return pl.BlockSpec((D_MODEL, width), lambda *_: (0, offset // width), pipeline_mode=pl.Buffered(1))


def _qkv_call(x, mod3, norm_w, w_in_bf, q_norm_w, k_norm_w, posf, freq2, perms):
    bsz, seq, _ = x.shape
    tm = TM_QKV
    tok = lambda b, s: (b, s, 0)
    dils = [d for _, d in DILATED_GROUPS]
    qkv_width = N_GROUPS * GROUP_WIDTH
    return pl.pallas_call(
        _qkv_kernel,
        grid=(bsz, seq // tm),
        in_specs=[pl.BlockSpec((1, tm, D_MODEL), tok),
                  pl.BlockSpec((1, 1, D_MODEL), lambda b, s: (3 * b, 0, 0)),
                  pl.BlockSpec((1, 1, D_MODEL), lambda b, s: (3 * b + 1, 0, 0)),
                  _const_spec((1, D_MODEL)),
                  _weight_cols_spec(OFF_Q, qkv_width),
                  _weight_cols_spec(OFF_K, qkv_width),
                  _weight_cols_spec(OFF_V, qkv_width),
                  _const_spec((1, HEAD_DIM)),
                  _const_spec((1, HEAD_DIM)),
                  pl.BlockSpec((1, tm, 1), tok),
                  _const_spec((1, HEAD_DIM)),
                  _const_spec(perms.shape)],
        out_specs=[_grouped_tile_spec(tm, d, GROUP_WIDTH) for d in dils] * 3,
        out_shape=[jax.ShapeDtypeStruct(_grouped_shape(bsz, seq, d, GROUP_WIDTH), jnp.bfloat16)
                   for d in dils] * 3,
        scratch_shapes=[pltpu.VMEM((3 * N_GROUPS, tm, GROUP_WIDTH), jnp.float32)],
        compiler_params=_params(2),
        name="qkv",
    )(x, mod3, mod3, norm_w.reshape(1, D_MODEL), w_in_bf, w_in_bf, w_in_bf,
      q_norm_w.reshape(1, HEAD_DIM), k_norm_w.reshape(1, HEAD_DIM), posf, freq2, perms)


def _attn_kernel(q_ref, k_ref, v_ref, kp_ref, vp_ref, cap_ref, o_ref, l_ref, *, nb_step, res_step):
    no_prev = (pl.program_id(1) == 0).astype(jnp.int32)
    ones =jnp.ones((BLK, HEAD_DIM), jnp.bfloat16)
    lane = lax.broadcasted_iota(jnp.int32, (BLK, HEAD_DIM), 1)
    nt_dims = (((1,), (1,)), ((), ()))

    def tile(ref, chunk0, r, j):
        t = ref[0, pl.ds(chunk0, CHUNKS_PER_BLK), r, :, j * HEAD_DIM:(j + 1) * HEAD_DIM]
        return t.reshape(BLK, HEAD_DIM)

    def process(pairs, first):
        units = [(i, r, j) for i, r in pairs for j in range(HEADS_PER_GROUP)]
        cap = cap_ref[no_prev] if first else cap_ref[0]
        q = [tile(q_ref, i * CHUNKS_PER_BLK, r, j) for i, r, j in units]
        kc = [tile(k_ref, i * CHUNKS_PER_BLK, r, j) for i, r, j in units]
        vc = [tile(v_ref, i * CHUNKS_PER_BLK, r, j) for i, r, j in units]
        if first:
            kp = [tile(kp_ref, 0, r, j) for _, r, j in units]
            vp = [tile(vp_ref, 0, r, j) for _, r, j in units]
        else:
            kp = [tile(k_ref, (i - 1) * CHUNKS_PER_BLK, r, j) for i, r, j in units]
            vp = [tile(v_ref, (i - 1) * CHUNKS_PER_BLK, r, j) for i, r, j in units]
        s = [lax.dot_general(a, jnp.concatenate([b, c], axis=0), nt_dims, preferred_element_type=jnp.float32)
             for a, b, c in zip(q, kp, kc)]
        s = [jnp.minimum(t, cap) for t in s]
        m = [jnp.max(t, axis=-1, keepdims=True) for t in s]
        p = [jnp.exp2(t - mm).astype(jnp.bfloat16) for t, mm in zip(s, m)]
        oe = [_bdot(t, jnp.concatenate([jnp.concatenate([va, ones], axis=-1),
                                        jnp.concatenate([vb, ones], axis=-1)], axis=0))
              for t, va, vb in zip(p, vp, vc)]
        for n, (i, r) in enumerate(pairs):
            chunks = pl.ds(i * CHUNKS_PER_BLK, CHUNKS_PER_BLK)
            stats = jnp.zeros((BLK, HEAD_DIM), jnp.float32)
            for j in range(HEADS_PER_GROUP):
                u = n * HEADS_PER_GROUP + j
                o_ref[0, chunks, r, :, j * HEAD_DIM:(j + 1) * HEAD_DIM] = (
                    oe[u][:, :HEAD_DIM].astype(o_ref.dtype).reshape(CHUNKS_PER_BLK, SUB_ROWS, HEAD_DIM))
                stats = jnp.where(lane == j, m[u], stats)
                stats = jnp.where(lane == HEADS_PER_GROUP + j, oe[u][:, HEAD_DIM:], stats)
            l_ref[0, chunks, r, :, :] = stats.reshape(CHUNKS_PER_BLK, SUB_ROWS, HEAD_DIM)

    def run(n_pairs, pair_of, first):
        left = n_pairs % PAIRS_PER_BODY
        if left:
            process([pair_of(t) for t in range(left)], first)
        if n_pairs >= PAIRS_PER_BODY:
            def body(t, carry):
                t0 = left + PAIRS_PER_BODY * t
                process([pair_of(t0 + n) for n in range(PAIRS_PER_BODY)], first)
                return carry
            lax.fori_loop(0, n_pairs // PAIRS_PER_BODY, body, 0)

    shift = res_step.bit_length() - 1

    def later_pair(t):
        return 1 + (t >> shift), t & (res_step - 1)

    run(res_step, lambda t: (0, t), True)
    run((nb_step - 1) * res_step, later_pair, False)


def _attn_call(q, k, v, caps, dilation):
    bsz, n_chunks = q.shape[:2]
    nb = n_chunks // CHUNKS_PER_BLK
    nb_step = min(nb, ATTN_PAIRS)
    res_step = min(ATTN_PAIRS // nb_step, dilation)
    assert res_step & (res_step - 1) == 0 and nb % nb_step == 0 and dilation % res_step == 0

    def cur(width):
        return pl.BlockSpec((1, nb_step * CHUNKS_PER_BLK, res_step, SUB_ROWS, width),
                            lambda b, n, rc: (b, n, rc, 0, 0))
    prev = pl.BlockSpec((1, CHUNKS_PER_BLK, res_step, SUB_ROWS, GROUP_WIDTH),
                        lambda b, n, rc: (b, jnp.maximum(n * nb_step - 1, 0), rc, 0, 0))
    qkv_spec = cur(GROUP_WIDTH)
    return pl.pallas_call(
        functools.partial(_attn_kernel, nb_step=nb_step, res_step=res_step),
        grid=(bsz, nb // nb_step, dilation // res_step),
        in_specs=[qkv_spec, qkv_spec, qkv_spec, prev, prev, _const_spec(caps.shape)],
        out_specs=[qkv_spec, cur(HEAD_DIM)],
        out_shape=[jax.ShapeDtypeStruct(q.shape, jnp.bfloat16),
                   jax.ShapeDtypeStruct(q.shape[:-1] + (HEAD_DIM,), jnp.float32)],
        compiler_params=_params(3),
        name=f"attn_d{dilation}",
    )(q, k, v, k, v, caps)


def _tail_kernel(x_ref, shift_ref, scale_ref, gate_ref, nw_ref,
                 o0_ref, o1_ref, o2_ref, l0_ref, l1_ref, l2_ref, pinv_ref, pinv32_ref,
                 wa_ref, wb_ref, wgc_ref, wga_ref, wmc_ref, wma_ref,
                 cw_ref, cb_ref, lnw_ref, lnb_ref, wco_ref, wao_ref, wout_ref,
                 out_ref, useq_ref, cseq_ref, side_ref):
    tm = x_ref.shape[1]
    nsl = D_MODEL // LANES

    @pl.when(pl.program_id(1) == 0)
    def _():
        useq_ref[0:HALO * nsl, :] = jnp.zeros((HALO * nsl, LANES), jnp.float32)

    x = x_ref[0]
    h = _normed_input(x, nw_ref[...], scale_ref[0], shift_ref[0]).astype(jnp.bfloat16)

    for c0 in range(0, D_MODEL, SIDE_PIECE):
        u = _bdot(h, wa_ref[:, c0:c0 + SIDE_PIECE]) * _sigmoid(_bdot(h, wb_ref[:, c0:c0 + SIDE_PIECE]))
        for s in range(SIDE_PIECE // LANES):
            row0 = HALO * nsl + c0 // LANES + s
            useq_ref[pl.ds(row0, tm, stride=nsl), :] = u[:, s * LANES:(s + 1) * LANES]
    side_jobs, col = [], 0
    for w_ref in (wgc_ref, wga_ref, wmc_ref, wma_ref):
        side_jobs += [(w_ref, c0, col + c0) for c0 in range(0, w_ref.shape[1], SIDE_PIECE)]
        col += w_ref.shape[1]
    n_chunks = tm // CONV_CHUNK
    parked = {}
    for c in range(n_chunks):
        base = c * CONV_CHUNK * nsl
        bias = cb_ref[...]
        for dst in parked.get(c - SIDE_LAG, ()):
            bias = bias + _zero_from(side_ref[0:nsl, dst:dst + LANES])
        acc = jnp.broadcast_to(bias[None], (CONV_CHUNK, nsl, LANES))
        for t in range(CONV_KERNEL):
            start = base + (HALO - (CONV_KERNEL - 1) + t) * nsl
            taps = useq_ref[pl.ds(start, CONV_CHUNK * nsl), :].reshape(CONV_CHUNK, nsl, LANES)
            acc = acc + cw_ref[t][None] * taps
        cseq_ref[pl.ds(base, CONV_CHUNK * nsl), :] = acc.reshape(CONV_CHUNK * nsl, LANES)
        lo, hi = len(side_jobs) * c // n_chunks, len(side_jobs) * (c + 1) // n_chunks
        for w_ref, c0, dst in side_jobs[lo:hi]:
            side_ref[:, dst:dst + SIDE_PIECE] = _bdot(h, w_ref[:, c0:c0 + SIDE_PIECE])
            parked.setdefault(c, []).append(dst)
    z_gc = side_ref[:, 0:D_MODEL]
    z_ga = side_ref[:, D_MODEL:D_MODEL + GROUP_WIDTH]
    z_mc = side_ref[:, D_MODEL + GROUP_WIDTH:2 * D_MODEL + GROUP_WIDTH]
    z_ma = side_ref[:, 2 * D_MODEL + GROUP_WIDTH:3 * D_MODEL + GROUP_WIDTH]
    useq_ref[0:HALO * nsl, :] = useq_ref[tm * nsl:(tm + HALO) * nsl, :]
    acc = jnp.concatenate([cseq_ref[pl.ds(s, tm, stride=nsl), :] for s in range(nsl)], axis=-1)
    mu = jnp.mean(acc, axis=-1, keepdims=True)
    cen = acc - mu
    var = jnp.mean(cen * cen, axis=-1, keepdims=True)
    ln = cen * lax.rsqrt(var + EPS) * lnw_ref[...] + lnb_ref[...]
    tc = (_silu(ln) * _silu(z_gc)).astype(jnp.bfloat16)
    y_conv = _bdot(tc, wco_ref[...])

    def token_order(ref, g, pmat_ref, **kw):
        rows = ref[0].reshape(tm, ref.shape[-1])
        if g == 0:
            return rows.astype(jnp.float32)
        return jnp.concatenate(
            [jnp.dot(pmat_ref[g - 1], rows[lo:lo + PERM_ROWS], preferred_element_type=jnp.float32, **kw)
             for lo in range(0, tm, PERM_ROWS)], axis=0)

    o_g = [token_order(r, g, pinv_ref) for g, r in enumerate((o0_ref, o1_ref, o2_ref))]
    stats = [token_order(r, g, pinv32_ref, precision=lax.Precision.HIGHEST)
             for g, r in enumerate((l0_ref, l1_ref, l2_ref))]
    top = jnp.maximum(jnp.maximum(stats[0], stats[1]), stats[2])
    e_g = [jnp.exp2(st - top) for st in stats]
    dens = [pltpu.roll(st, LANES - HEADS_PER_GROUP, axis=1) for st in stats]
    total = e_g[0] * dens[0] + e_g[1] * dens[1] + e_g[2] * dens[2]
    head_lane = lax.broadcasted_iota(jnp.int32, total.shape, 1) < HEADS_PER_GROUP
    inv = 1.0 / jnp.where(head_lane, total, 1.0)
    w_g = [e * inv for e in e_g]
    heads = []
    for j in range(HEADS_PER_GROUP):
        cols = slice(j * HEAD_DIM, (j + 1) * HEAD_DIM)
        heads.append(sum(w[:, j:j + 1] * o[:, cols] for w, o in zip(w_g, o_g)))
    o = jnp.concatenate(heads, axis=-1)
    ta = (o * _silu(z_ga)).astype(jnp.bfloat16)
    y_attn = _bdot(ta, wao_ref[...])

    y = _sigmoid(z_mc) * y_conv + _sigmoid(z_ma) * y_attn
    out = _bdot(y.astype(jnp.bfloat16), wout_ref[...])
    out_ref[0] = x + gate_ref[0] * out


def _tail_call(x, mod3, norm_w, o_groups, l_groups, pinv, pinv32, w_in_bf, conv_w, conv_b, ln_w, ln_b,
               w_co, w_ao, w_out):
    side_cols = ((OFF_A, D_MODEL), (OFF_B, D_MODEL), (OFF_GC, D_MODEL),
                 (OFF_GA, GROUP_WIDTH), (OFF_MC, D_MODEL), (OFF_MA, D_MODEL))
    bsz, seq, _ = x.shape
    tm = TM_TAIL
    nsl = D_MODEL // LANES
    tok = lambda b, s: (b, s, 0)
    row = lambda v: v.reshape(1, D_MODEL)
    modspec = lambda k: pl.BlockSpec((1, 1, D_MODEL), lambda b, s: (3 * b + k, 0, 0))
    dils = [d for _, d in DILATED_GROUPS]
    in_specs = ([pl.BlockSpec((1, tm, D_MODEL), tok), modspec(0), modspec(1), modspec(2),
                 _const_spec((1, D_MODEL))]
                + [_grouped_tile_spec(tm, d, GROUP_WIDTH) for d in dils]
                + [_grouped_tile_spec(tm, d, HEAD_DIM) for d in dils]
                + [_const_spec(pinv.shape), _const_spec(pinv32.shape)]
                + [_weight_cols_spec(off, width) for off, width in side_cols]
                + [_const_spec((CONV_KERNEL, nsl, LANES)), _const_spec((nsl, LANES))]
                + [_const_spec((1, D_MODEL))] * 2
                + [_const_spec(w_co.shape), _const_spec(w_ao.shape), _const_spec(w_out.shape)])
    return pl.pallas_call(
        _tail_kernel,
        grid=(bsz, seq // tm),
        in_specs=in_specs,
        out_specs=pl.BlockSpec((1, tm, D_MODEL), tok),
        out_shape=jax.ShapeDtypeStruct(x.shape, x.dtype),
        scratch_shapes=[pltpu.VMEM(((tm + HALO) * nsl, LANES), jnp.float32),
                        pltpu.VMEM((tm * nsl, LANES), jnp.float32),
                        pltpu.VMEM((tm, 3 * D_MODEL + GROUP_WIDTH), jnp.float32)],
        compiler_params=_params(2),
        name="tail",
    )(x, mod3, mod3, mod3, row(norm_w), *o_groups, *l_groups, pinv, pinv32, *([w_in_bf] * len(side_cols)),
      conv_w.reshape(CONV_KERNEL, nsl, LANES), conv_b.reshape(nsl, LANES), row(ln_w), row(ln_b),
      w_co, w_ao, w_out)


def _attention_caps():
    qi = np.arange(BLK)[:, None]
    kj = np.arange(BLK)[None, :]
    prev = np.where(kj >= qi, BIG, NEG_INF)
    cur = np.where(kj <= qi, BIG, NEG_INF)
    no_prev = np.full((BLK, BLK), NEG_INF)
    caps = np.stack([np.concatenate([prev, cur], axis=1),
                     np.concatenate([no_prev, cur], axis=1)]).astype(np.float32)
    return jnp.asarray(caps)


def kernel(x, c, positions, norm_w, w_ada, b_ada, w_in, conv_w, conv_b, conv_ln_w, conv_ln_b,
           w_conv_out, q_norm_w, k_norm_w, w_attn_out, w_out):
    bsz, seq, _ = x.shape
    bf = jnp.bfloat16
    for window, dilation in DILATED_GROUPS:
        assert window // dilation == BLK and seq % (dilation * BLK) == 0
        assert PERM_ROWS % (SUB_ROWS * dilation) == 0
    assert seq % TM_QKV == 0 and seq % TM_TAIL == 0
    assert TM_QKV % PERM_ROWS == 0 and TM_TAIL % PERM_ROWS == 0

    mod3 = _mod_call(c, w_ada, b_ada).reshape(bsz * 3, 1, D_MODEL)

    w_in_bf = w_in.astype(bf)
    inv_freq = ROPE_THETA ** (-jnp.arange(0, HEAD_DIM, 2, dtype=jnp.float32) / HEAD_DIM)
    freq2 = jnp.concatenate([inv_freq, inv_freq]).reshape(1, HEAD_DIM)
    posf = positions.astype(jnp.float32)[..., None]
    perm_np = np.stack([_residue_perm(d) for _, d in DILATED_GROUPS[1:]])
    perms = jnp.asarray(perm_np, bf)
    pinv32 = jnp.asarray(perm_np.transpose(0, 2, 1))
    qkv = _qkv_call(x, mod3, norm_w, w_in_bf, q_norm_w, k_norm_w, posf, freq2, perms)

    caps = _attention_caps()
    o_groups, l_groups = [], []
    for g, (_, dilation) in enumerate(DILATED_GROUPS):
        o_g, l_g = _attn_call(qkv[g], qkv[N_GROUPS + g], qkv[2 * N_GROUPS + g], caps, dilation)
        o_groups.append(o_g)
        l_groups.append(l_g)

    return _tail_call(x, mod3, norm_w, o_groups, l_groups, pinv32.astype(bf), pinv32, w_in_bf,
                      conv_w, conv_b, conv_ln_w, conv_ln_b,
                      w_conv_out.astype(bf), w_attn_out.astype(bf), w_out.astype(bf))
```

```python
import functools
import math

import jax
import jax.numpy as jnp
import numpy as np
from jax import lax
from jax.experimental import pallas as pl
from jax.experimental.pallas import tpu as pltpu

D_MODEL = 1024
CONV_KERNEL = 31
HEAD_DIM = 128
HEADS_PER_GROUP = 4
GROUP_WIDTH = HEADS_PER_GROUP * HEAD_DIM
DILATED_GROUPS = ((128, 1), (512, 4), (2048, 16))
N_GROUPS = len(DILATED_GROUPS)
BLK = 128
ROPE_THETA = 10000.0
EPS = 1e-6
NEG_INF = -1e30
BIG = 3.0e38
LOG2E = math.log2(math.e)

OFF_A, OFF_B, OFF_GC = 0, 1024, 2048
OFF_Q, OFF_K, OFF_V = 3072, 4608, 6144
OFF_GA, OFF_MC, OFF_MA = 7680, 8192, 9216

LANES = 128
SUB_ROWS = 16
CHUNKS_PER_BLK = BLK // SUB_ROWS
PERM_ROWS = 256
GATHER_STRIDE = 4
HALO = 32
CONV_CHUNK = 8
SIDE_PIECE = 256
SIDE_LAG = 8
VMEM_LIMIT = 56 * 1024 * 1024

TM_QKV = 512
TM_TAIL = 512
PROJ_AHEAD = 1
ATTN_PAIRS = 16
PAIRS_PER_BODY = 4


def _params(n_axes):
    return pltpu.CompilerParams(dimension_semantics=("arbitrary",) * n_axes,
                                vmem_limit_bytes=VMEM_LIMIT)


def _const_spec(shape):
    nd = len(shape)
    return pl.BlockSpec(shape, lambda *_: (0,) * nd, pipeline_mode=pl.Buffered(1))


def _sigmoid(v):
    return jax.nn.sigmoid(v)


def _silu(v):
    return v * jax.nn.sigmoid(v)


def _bdot(a, b):
    return jnp.dot(a, b, preferred_element_type=jnp.float32)


def _zero_from(v):
    bits = pltpu.bitcast(v, jnp.uint32)
    return pltpu.bitcast((bits >> 16) >> 16, jnp.float32)


def _residue_perm(dilation):
    chunk = SUB_ROWS * dilation
    p = np.zeros((PERM_ROWS, PERM_ROWS), np.float32)
    for a in range(PERM_ROWS // chunk):
        for r in range(dilation):
            for j in range(SUB_ROWS):
                p[a * chunk + r * SUB_ROWS + j, a * chunk + j * dilation + r] = 1.0
    return p


def _mod_kernel(c_ref, w_ref, b_ref, o_ref):
    sc = _silu(c_ref[...])
    o_ref[...] = jnp.dot(sc, w_ref[...], preferred_element_type=jnp.float32,
                         precision=lax.Precision.HIGHEST) + b_ref[...]


def _mod_call(c, w_ada, b_ada):
    bsz = c.shape[0]
    n_out = w_ada.shape[1]
    bn = 1024
    return pl.pallas_call(
        _mod_kernel,
        grid=(n_out // bn,),
        in_specs=[pl.BlockSpec((bsz, D_MODEL), lambda j: (0, 0)),
                  pl.BlockSpec((D_MODEL, bn), lambda j: (0, j)),
                  pl.BlockSpec((1, bn), lambda j: (0, j))],
        out_specs=pl.BlockSpec((bsz, bn), lambda j: (0, j)),
        out_shape=jax.ShapeDtypeStruct((bsz, n_out), jnp.float32),
        compiler_params=_params(1),
        name="mod",
    )(c, w_ada, b_ada.reshape(1, n_out))


def _normed_input(x, norm_w, scale, shift):
    ms = jnp.mean(x * x, axis=-1, keepdims=True)
    return x * lax.rsqrt(ms + EPS) * norm_w * (1.0 + scale) + shift


def _grouped_shape(bsz, seq, dilation, width):
    return (bsz, seq // (SUB_ROWS * dilation), dilation, SUB_ROWS, width)


def _grouped_tile_spec(tm, dilation, width):
    return pl.BlockSpec((1, tm // (SUB_ROWS * dilation), dilation, SUB_ROWS, width),
                        lambda b, s: (b, s, 0, 0, 0))


def _residue_major(val, dilation, buf_a, buf_b):
    tm = val.shape[0]
    if dilation == 1:
        return val
    buf_a[...] = val
    if dilation == GATHER_STRIDE:
        chunk = SUB_ROWS * dilation
        return jnp.concatenate(
            [buf_a[pl.ds(a * chunk + r, SUB_ROWS, stride=dilation), :]
             for a in range(tm // chunk) for r in range(dilation)], axis=0)
    assert dilation == GATHER_STRIDE * GATHER_STRIDE and PERM_ROWS == SUB_ROWS * dilation
    quarter = PERM_ROWS // GATHER_STRIDE
    for blk in range(tm // PERM_ROWS):
        for r0 in range(GATHER_STRIDE):
            buf_b[blk * PERM_ROWS + r0 * quarter:blk * PERM_ROWS + (r0 + 1) * quarter, :] = (
                buf_a[pl.ds(blk * PERM_ROWS + r0, quarter, stride=GATHER_STRIDE), :])
    return jnp.concatenate(
        [buf_b[pl.ds(blk * PERM_ROWS + (r % GATHER_STRIDE) * quarter + r // GATHER_STRIDE, SUB_ROWS,
                     stride=GATHER_STRIDE), :]
         for blk in range(tm // PERM_ROWS) for r in range(dilation)], axis=0)


def _qkv_kernel(x_ref, shift_ref, scale_ref, nw_ref, wq_ref, wk_ref, wv_ref, qnw_ref, knw_ref, pos_ref,
                freq_ref, *out_and_scratch):
    *out_refs, z_ref, buf_a_ref, buf_b_ref, h_ref = out_and_scratch
    w_refs = (wq_ref, wk_ref, wv_ref)
    x = x_ref[0]
    tm = x.shape[0]
    h_ref[...] = _normed_input(x, nw_ref[...], scale_ref[0], shift_ref[0]).astype(jnp.bfloat16)
    half_rows = tm // 2
    half = HEAD_DIM // 2
    pos = pos_ref[0]
    lane = lax.broadcasted_iota(jnp.int32, (half_rows, HEAD_DIM), 1)
    low = lane < half
    ang = jnp.where(low, pos[:half_rows], pos[half_rows:]) * freq_ref[...]
    cos_p, sin_p = jnp.cos(ang), jnp.sin(ang)
    cos_s, sin_s = pltpu.roll(cos_p, half, axis=1), pltpu.roll(sin_p, half, axis=1)
    cos_t = jnp.concatenate([jnp.where(low, cos_p, cos_s), jnp.where(low, cos_s, cos_p)], axis=0)
    sin_t = jnp.concatenate([jnp.where(low, -sin_p, sin_s), jnp.where(low, -sin_s, sin_p)], axis=0)
    norm_ws = (qnw_ref[...] * (LOG2E * HEAD_DIM ** -0.5), knw_ref[...])
    n_proj = 3 * N_GROUPS

    def project(idx):
        kind, g = divmod(idx, N_GROUPS)
        z_ref[idx] = _bdot(h_ref[...], w_refs[kind][:, g * GROUP_WIDTH:(g + 1) * GROUP_WIDTH])

    for idx in range(PROJ_AHEAD):
        project(idx)
    for idx in range(n_proj):
        kind, g = divmod(idx, N_GROUPS)
        dilation = DILATED_GROUPS[g][1]
        heads = []
        for j in range(HEADS_PER_GROUP):
            zh = z_ref[idx, :, j * HEAD_DIM:(j + 1) * HEAD_DIM]
            if kind < 2:
                ms = jnp.mean(zh * zh, axis=-1, keepdims=True)
                zn = zh * lax.rsqrt(ms + EPS) * norm_ws[kind]
                zh = zn * cos_t + pltpu.roll(zn, half, axis=1) * sin_t
            heads.append(_residue_major(zh, dilation, buf_a_ref.at[j], buf_b_ref.at[j]).astype(jnp.bfloat16))
        o_ref = out_refs[idx]
        o_ref[0] = jnp.concatenate(heads, axis=-1).reshape(o_ref.shape[1:])
        if idx + PROJ_AHEAD < n_proj:
            done = pltpu.bitcast(o_ref[0, 0, 0, :, 0:LANES], jnp.uint32)
            words = pltpu.bitcast(h_ref[0:SUB_ROWS, 0:LANES], jnp.uint32) + ((done >> 16) >> 16)
            h_ref[0:SUB_ROWS, 0:LANES] = pltpu.bitcast(words, jnp.bfloat16)
            project(idx + PROJ_AHEAD)


def _weight_cols_spec(offset, width):
    assert offset % width == 0
    return pl.BlockSpec((D_MODEL, width), lambda *_: (0, offset // width), pipeline_mode=pl.Buffered(1))


def _qkv_call(x, mod3, norm_w, w_in_bf, q_norm_w, k_norm_w, posf, freq2):
    bsz, seq, _ = x.shape
    tm = TM_QKV
    tok = lambda b, s: (b, s, 0)
    dils = [d for _, d in DILATED_GROUPS]
    qkv_width = N_GROUPS * GROUP_WIDTH
    return pl.pallas_call(
        _qkv_kernel,
        grid=(bsz, seq // tm),
        in_specs=[pl.BlockSpec((1, tm, D_MODEL), tok),
                  pl.BlockSpec((1, 1, D_MODEL), lambda b, s: (3 * b, 0, 0)),
                  pl.BlockSpec((1, 1, D_MODEL), lambda b, s: (3 * b + 1, 0, 0)),
                  _const_spec((1, D_MODEL)),
                  _weight_cols_spec(OFF_Q, qkv_width),
                  _weight_cols_spec(OFF_K, qkv_width),
                  _weight_cols_spec(OFF_V, qkv_width),
                  _const_spec((1, HEAD_DIM)),
                  _const_spec((1, HEAD_DIM)),
                  pl.BlockSpec((1, tm, 1), tok),
                  _const_spec((1, HEAD_DIM))],
        out_specs=[_grouped_tile_spec(tm, d, GROUP_WIDTH) for d in dils] * 3,
        out_shape=[jax.ShapeDtypeStruct(_grouped_shape(bsz, seq, d, GROUP_WIDTH), jnp.bfloat16)
                   for d in dils] * 3,
        scratch_shapes=[pltpu.VMEM((3 * N_GROUPS, tm, GROUP_WIDTH), jnp.float32),
                        pltpu.VMEM((HEADS_PER_GROUP, tm, HEAD_DIM), jnp.float32),
                        pltpu.VMEM((HEADS_PER_GROUP, tm, HEAD_DIM), jnp.float32),
                        pltpu.VMEM((tm, D_MODEL), jnp.bfloat16)],
        compiler_params=_params(2),
        name="qkv",
    )(x, mod3, mod3, norm_w.reshape(1, D_MODEL), w_in_bf, w_in_bf, w_in_bf,
      q_norm_w.reshape(1, HEAD_DIM), k_norm_w.reshape(1, HEAD_DIM), posf, freq2)


def _attn_kernel(q_ref, k_ref, v_ref, kp_ref, vp_ref, cap_ref, o_ref, l_ref, *, nb_step, res_step):
    no_prev = (pl.program_id(1) == 0).astype(jnp.int32)
    ones = jnp.ones((BLK, HEAD_DIM), jnp.bfloat16)
    lane = lax.broadcasted_iota(jnp.int32, (BLK, HEAD_DIM), 1)
    nt_dims = (((1,), (1,)), ((), ()))

    def tile(ref, chunk0, r, j):
        t = ref[0, pl.ds(chunk0, CHUNKS_PER_BLK), r, :, j * HEAD_DIM:(j + 1) * HEAD_DIM]
        return t.reshape(BLK, HEAD_DIM)

    def process(pairs, first):
        units = [(i, r, j) for i, r in pairs for j in range(HEADS_PER_GROUP)]
        cap = cap_ref[no_prev] if first else cap_ref[0]
        q = [tile(q_ref, i * CHUNKS_PER_BLK, r, j) for i, r, j in units]
        kc = [tile(k_ref, i * CHUNKS_PER_BLK, r, j) for i, r, j in units]
        vc = [tile(v_ref, i * CHUNKS_PER_BLK, r, j) for i, r, j in units]
        if first:
            kp = [tile(kp_ref, 0, r, j) for _, r, j in units]
            vp = [tile(vp_ref, 0, r, j) for _, r, j in units]
        else:
            kp = [tile(k_ref, (i - 1) * CHUNKS_PER_BLK, r, j) for i, r, j in units]
            vp = [tile(v_ref, (i - 1) * CHUNKS_PER_BLK, r, j) for i, r, j in units]
        s = [lax.dot_general(a, jnp.concatenate([b, c], axis=0), nt_dims, preferred_element_type=jnp.float32)
             for a, b, c in zip(q, kp, kc)]
        s = [jnp.minimum(t, cap) for t in s]
        m = [jnp.max(t, axis=-1, keepdims=True) for t in s]
        p = [jnp.exp2(t - mm).astype(jnp.bfloat16) for t, mm in zip(s, m)]
        oe = [_bdot(t, jnp.concatenate([jnp.concatenate([va, ones], axis=-1),
                                        jnp.concatenate([vb, ones], axis=-1)], axis=0))
              for t, va, vb in zip(p, vp, vc)]
        for n, (i, r) in enumerate(pairs):
            chunks = pl.ds(i * CHUNKS_PER_BLK, CHUNKS_PER_BLK)
            stats = jnp.zeros((BLK, HEAD_DIM), jnp.float32)
            for j in range(HEADS_PER_GROUP):
                u = n * HEADS_PER_GROUP + j
                o_ref[0, chunks, r, :, j * HEAD_DIM:(j + 1) * HEAD_DIM] = (
                    oe[u][:, :HEAD_DIM].astype(o_ref.dtype).reshape(CHUNKS_PER_BLK, SUB_ROWS, HEAD_DIM))
                stats = jnp.where(lane == j, m[u], stats)
                stats = jnp.where(lane == HEADS_PER_GROUP + j, oe[u][:, HEAD_DIM:], stats)
            l_ref[0, chunks, r, :, :] = stats.reshape(CHUNKS_PER_BLK, SUB_ROWS, HEAD_DIM)

    def run(n_pairs, pair_of, first):
        left = n_pairs % PAIRS_PER_BODY
        if left:
            process([pair_of(t) for t in range(left)], first)
        if n_pairs >= PAIRS_PER_BODY:
            def body(t, carry):
                t0 = left + PAIRS_PER_BODY * t
                process([pair_of(t0 + n) for n in range(PAIRS_PER_BODY)], first)
                return carry
            lax.fori_loop(0, n_pairs // PAIRS_PER_BODY, body, 0)

    shift = res_step.bit_length() - 1

    def later_pair(t):
        return 1 + (t >> shift), t & (res_step - 1)

    run(res_step, lambda t: (0, t), True)
    run((nb_step - 1) * res_step, later_pair, False)


def _attn_call(q, k, v, caps, dilation):
    bsz, n_chunks = q.shape[:2]
    nb = n_chunks // CHUNKS_PER_BLK
    nb_step = min(nb, ATTN_PAIRS)
    res_step = min(ATTN_PAIRS // nb_step, dilation)
    assert res_step & (res_step - 1) == 0 and nb % nb_step == 0 and dilation % res_step == 0

    def cur(width):
        return pl.BlockSpec((1, nb_step * CHUNKS_PER_BLK, res_step, SUB_ROWS, width),
                            lambda b, n, rc: (b, n, rc, 0, 0))
    prev = pl.BlockSpec((1, CHUNKS_PER_BLK, res_step, SUB_ROWS, GROUP_WIDTH),
                        lambda b, n, rc: (b, jnp.maximum(n * nb_step - 1, 0), rc, 0, 0))
    qkv_spec = cur(GROUP_WIDTH)
    return pl.pallas_call(
        functools.partial(_attn_kernel, nb_step=nb_step, res_step=res_step),
        grid=(bsz, nb // nb_step, dilation // res_step),
        in_specs=[qkv_spec, qkv_spec, qkv_spec, prev, prev, _const_spec(caps.shape)],
        out_specs=[qkv_spec, cur(HEAD_DIM)],
        out_shape=[jax.ShapeDtypeStruct(q.shape, jnp.bfloat16),
                   jax.ShapeDtypeStruct(q.shape[:-1] + (HEAD_DIM,), jnp.float32)],
        compiler_params=_params(3),
        name=f"attn_d{dilation}",
    )(q, k, v, k, v, caps)


def _tail_kernel(x_ref, shift_ref, scale_ref, gate_ref, nw_ref,
                 o0_ref, o1_ref, o2_ref, l0_ref, l1_ref, l2_ref, pinv_ref, pinv32_ref,
                 wa_ref, wb_ref, wgc_ref, wga_ref, wmc_ref, wma_ref,
                 cw_ref, cb_ref, lnw_ref, lnb_ref, wco_ref, wao_ref, wout_ref,
                 out_ref, useq_ref, cseq_ref, side_ref, otok_ref, stok_ref, tc_ref):
    tm = x_ref.shape[1]
    nsl = D_MODEL // LANES

    @pl.when(pl.program_id(1) == 0)
    def _():
        useq_ref[0:HALO * nsl, :] = jnp.zeros((HALO * nsl, LANES), jnp.float32)

    x = x_ref[0]
    h = _normed_input(x, nw_ref[...], scale_ref[0], shift_ref[0]).astype(jnp.bfloat16)

    for c0 in range(0, D_MODEL, SIDE_PIECE):
        u = _bdot(h, wa_ref[:, c0:c0 + SIDE_PIECE]) * _sigmoid(_bdot(h, wb_ref[:, c0:c0 + SIDE_PIECE]))
        for s in range(SIDE_PIECE // LANES):
            row0 = HALO * nsl + c0 // LANES + s
            useq_ref[pl.ds(row0, tm, stride=nsl), :] = u[:, s * LANES:(s + 1) * LANES]
    def side_piece(w_ref, c0, dst):
        side_ref[:, dst:dst + SIDE_PIECE] = _bdot(h, w_ref[:, c0:c0 + SIDE_PIECE])
        return lambda: side_ref[0:nsl, dst:dst + LANES]

    def token_order_job(src_ref, dst_ref, g, lo, pmat_ref, **kw):
        rows = src_ref[0].reshape(tm, src_ref.shape[-1])[lo:lo + PERM_ROWS]
        dst_ref[g - 1, lo:lo + PERM_ROWS, :] = jnp.dot(pmat_ref[g - 1], rows,
                                                        preferred_element_type=jnp.float32, **kw)
        return lambda: dst_ref[g - 1, lo:lo + nsl, 0:LANES]

    early_jobs, late_jobs, col = [], [], 0
    for w_ref, early in ((wgc_ref, True), (wga_ref, True), (wmc_ref, False), (wma_ref, False)):
        for c0 in range(0, w_ref.shape[1], SIDE_PIECE):
            (early_jobs if early else late_jobs).append(
                functools.partial(side_piece, w_ref, c0, col + c0))
        col += w_ref.shape[1]
    for g, (o_ref, l_ref) in enumerate(((o1_ref, l1_ref), (o2_ref, l2_ref)), start=1):
        for lo in range(0, tm, PERM_ROWS):
            early_jobs.append(functools.partial(token_order_job, o_ref, otok_ref, g, lo, pinv_ref))
            early_jobs.append(functools.partial(token_order_job, l_ref, stok_ref, g, lo, pinv32_ref,
                                                precision=lax.Precision.HIGHEST))
    n_chunks = tm // CONV_CHUNK
    parked = {}
    for c in range(n_chunks):
        base = c * CONV_CHUNK * nsl
        bias = cb_ref[...]
        for read_back in parked.get(c - SIDE_LAG, ()):
            bias = bias + _zero_from(read_back())
        acc = jnp.broadcast_to(bias[None], (CONV_CHUNK, nsl, LANES))
        for t in range(CONV_KERNEL):
            start = base + (HALO - (CONV_KERNEL - 1) + t) * nsl
            taps = useq_ref[pl.ds(start, CONV_CHUNK * nsl), :].reshape(CONV_CHUNK, nsl, LANES)
            acc = acc + cw_ref[t][None] * taps
        cseq_ref[pl.ds(base, CONV_CHUNK * nsl), :] = acc.reshape(CONV_CHUNK * nsl, LANES)
        lo, hi = len(early_jobs) * c // n_chunks, len(early_jobs) * (c + 1) // n_chunks
        parked[c] = [job() for job in early_jobs[lo:hi]]
    late_reads = [job() for job in late_jobs]
    z_gc = side_ref[:, 0:D_MODEL]
    z_ga = side_ref[:, D_MODEL:D_MODEL + GROUP_WIDTH]
    z_mc = side_ref[:, D_MODEL + GROUP_WIDTH:2 * D_MODEL + GROUP_WIDTH]
    z_ma = side_ref[:, 2 * D_MODEL + GROUP_WIDTH:3 * D_MODEL + GROUP_WIDTH]
    useq_ref[0:HALO * nsl, :] = useq_ref[tm * nsl:(tm + HALO) * nsl, :]
    acc = jnp.concatenate([cseq_ref[pl.ds(s, tm, stride=nsl), :] for s in range(nsl)], axis=-1)
    mu = jnp.mean(acc, axis=-1, keepdims=True)
    cen = acc - mu
    var = jnp.mean(cen * cen, axis=-1, keepdims=True)
    ln = cen * lax.rsqrt(var + EPS) * lnw_ref[...] + lnb_ref[...]
    tc_ref[...] = (_silu(ln) * _silu(z_gc)).astype(jnp.bfloat16)
    words = pltpu.bitcast(tc_ref[0:SUB_ROWS, 0:LANES], jnp.uint32)
    for read_back in late_reads:
        words = words + ((pltpu.bitcast(read_back(), jnp.uint32) >> 16) >> 16)
    tc_ref[0:SUB_ROWS, 0:LANES] = pltpu.bitcast(words, jnp.bfloat16)
    y_conv = _bdot(tc_ref[...], wco_ref[...])

    o_g = [o0_ref[0].reshape(tm, GROUP_WIDTH).astype(jnp.float32), otok_ref[0], otok_ref[1]]
    stats = [l0_ref[0].reshape(tm, HEAD_DIM), stok_ref[0], stok_ref[1]]
    top = jnp.maximum(jnp.maximum(stats[0], stats[1]), stats[2])
    e_g = [jnp.exp2(st - top) for st in stats]
    dens = [pltpu.roll(st, LANES - HEADS_PER_GROUP, axis=1) for st in stats]
    total = e_g[0] * dens[0] + e_g[1] * dens[1] + e_g[2] * dens[2]
    head_lane = lax.broadcasted_iota(jnp.int32, total.shape, 1) < HEADS_PER_GROUP
    inv = 1.0 / jnp.where(head_lane, total, 1.0)
    w_g = [e * inv for e in e_g]
    heads = []
    for j in range(HEADS_PER_GROUP):
        cols = slice(j * HEAD_DIM, (j + 1) * HEAD_DIM)
        heads.append(sum(w[:, j:j + 1] * o[:, cols] for w, o in zip(w_g, o_g)))
    o = jnp.concatenate(heads, axis=-1)
    ta = (o * _silu(z_ga)).astype(jnp.bfloat16)
    y_attn = _bdot(ta, wao_ref[...])

    y = _sigmoid(z_mc) * y_conv + _sigmoid(z_ma) * y_attn
    out = _bdot(y.astype(jnp.bfloat16), wout_ref[...])
    out_ref[0] = x + gate_ref[0] * out


def _tail_call(x, mod3, norm_w, o_groups, l_groups, pinv, pinv32, w_in_bf, conv_w, conv_b, ln_w, ln_b,
               w_co, w_ao, w_out):
    side_cols = ((OFF_A, D_MODEL), (OFF_B, D_MODEL), (OFF_GC, D_MODEL),
                 (OFF_GA, GROUP_WIDTH), (OFF_MC, D_MODEL), (OFF_MA, D_MODEL))
    bsz, seq, _ = x.shape
    tm = TM_TAIL
    nsl = D_MODEL // LANES
    tok = lambda b, s: (b, s, 0)
    row = lambda v: v.reshape(1, D_MODEL)
    modspec = lambda k: pl.BlockSpec((1, 1, D_MODEL), lambda b, s: (3 * b + k, 0, 0))
    dils = [d for _, d in DILATED_GROUPS]
    in_specs = ([pl.BlockSpec((1, tm, D_MODEL), tok), modspec(0), modspec(1), modspec(2),
                 _const_spec((1, D_MODEL))]
                + [_grouped_tile_spec(tm, d, GROUP_WIDTH) for d in dils]
                + [_grouped_tile_spec(tm, d, HEAD_DIM) for d in dils]
                + [_const_spec(pinv.shape), _const_spec(pinv32.shape)]
                + [_weight_cols_spec(off, width) for off, width in side_cols]
                + [_const_spec((CONV_KERNEL, nsl, LANES)), _const_spec((nsl, LANES))]
                + [_const_spec((1, D_MODEL))] * 2
                + [_const_spec(w_co.shape), _const_spec(w_ao.shape), _const_spec(w_out.shape)])
    return pl.pallas_call(
        _tail_kernel,
        grid=(bsz, seq // tm),
        in_specs=in_specs,
        out_specs=pl.BlockSpec((1, tm, D_MODEL), tok),
        out_shape=jax.ShapeDtypeStruct(x.shape, x.dtype),
        scratch_shapes=[pltpu.VMEM(((tm + HALO) * nsl, LANES), jnp.float32),
                        pltpu.VMEM((tm * nsl, LANES), jnp.float32),
                        pltpu.VMEM((tm, 3 * D_MODEL + GROUP_WIDTH), jnp.float32),
                        pltpu.VMEM((N_GROUPS - 1, tm, GROUP_WIDTH), jnp.float32),
                        pltpu.VMEM((N_GROUPS - 1, tm, HEAD_DIM), jnp.float32),
                        pltpu.VMEM((tm, D_MODEL), jnp.bfloat16)],
        compiler_params=_params(2),
        name="tail",
    )(x, mod3, mod3, mod3, row(norm_w), *o_groups, *l_groups, pinv, pinv32, *([w_in_bf] * len(side_cols)),
      conv_w.reshape(CONV_KERNEL, nsl, LANES), conv_b.reshape(nsl, LANES), row(ln_w), row(ln_b),
      w_co, w_ao, w_out)


def _attention_caps():
    qi = np.arange(BLK)[:, None]
    kj = np.arange(BLK)[None, :]
    prev = np.where(kj >= qi, BIG, NEG_INF)
    cur = np.where(kj <= qi, BIG, NEG_INF)
    no_prev = np.full((BLK, BLK), NEG_INF)
    caps = np.stack([np.concatenate([prev, cur], axis=1),
                     np.concatenate([no_prev, cur], axis=1)]).astype(np.float32)
    return jnp.asarray(caps)


def kernel(x, c, positions, norm_w, w_ada, b_ada, w_in, conv_w, conv_b, conv_ln_w, conv_ln_b,
           w_conv_out, q_norm_w, k_norm_w, w_attn_out, w_out):
    bsz, seq, _ = x.shape
    bf = jnp.bfloat16
    for window, dilation in DILATED_GROUPS:
        assert window // dilation == BLK and seq % (dilation * BLK) == 0
        assert PERM_ROWS % (SUB_ROWS * dilation) == 0
    assert seq % TM_QKV == 0 and seq % TM_TAIL == 0
    assert TM_QKV % PERM_ROWS == 0 and TM_TAIL % PERM_ROWS == 0

    mod3 = _mod_call(c, w_ada, b_ada).reshape(bsz * 3, 1, D_MODEL)

    w_in_bf = w_in.astype(bf)
    inv_freq = ROPE_THETA ** (-jnp.arange(0, HEAD_DIM, 2, dtype=jnp.float32) / HEAD_DIM)
    freq2 = jnp.concatenate([inv_freq, inv_freq]).reshape(1, HEAD_DIM)
    posf = positions.astype(jnp.float32)[..., None]
    qkv = _qkv_call(x, mod3, norm_w, w_in_bf, q_norm_w, k_norm_w, posf, freq2)

    caps = _attention_caps()
    o_groups, l_groups = [], []
    for g, (_, dilation) in enumerate(DILATED_GROUPS):
        o_g, l_g = _attn_call(qkv[g], qkv[N_GROUPS + g], qkv[2 * N_GROUPS + g], caps, dilation)
        o_groups.append(o_g)
        l_groups.append(l_g)

    pinv32 = jnp.asarray(np.stack([_residue_perm(d).T for _, d in DILATED_GROUPS[1:]]))
    return _tail_call(x, mod3, norm_w, o_groups, l_groups, pinv32.astype(bf), pinv32, w_in_bf,
                      conv_w, conv_b, conv_ln_w, conv_ln_b,
                      w_conv_out.astype(bf), w_attn_out.astype(bf), w_out.astype(bf))
```

```python
import functools
import math

import jax
import jax.numpy as jnp
import numpy as np
from jax import lax
from jax.experimental import pallas as pl
from jax.experimental.pallas import tpu as pltpu

D_MODEL = 1024
CONV_KERNEL = 31
HEAD_DIM = 128
HEADS_PER_GROUP = 4
GROUP_WIDTH = HEADS_PER_GROUP * HEAD_DIM
DILATED_GROUPS = ((128, 1), (512, 4), (2048, 16))
N_GROUPS = len(DILATED_GROUPS)
BLK = 128
ROPE_THETA = 10000.0
EPS = 1e-6
NEG_INF = -1e30
BIG = 3.0e38
LOG2E = math.log2(math.e)

OFF_A, OFF_B, OFF_GC = 0, 1024, 2048
OFF_Q, OFF_K, OFF_V = 3072, 4608, 6144
OFF_GA, OFF_MC, OFF_MA = 7680, 8192, 9216

LANES = 128
SUB_ROWS = 16
CHUNKS_PER_BLK = BLK // SUB_ROWS
PERM_ROWS = 256
GATHER_STRIDE = 4
HALO = 32
CONV_CHUNK = 8
SIDE_PIECE = 256
SIDE_LAG = 8
VMEM_LIMIT = 56 * 1024 * 1024

TM_QKV = 512
TM_TAIL = 512
PROJ_AHEAD = 1
ATTN_PAIRS = 16
PAIRS_PER_BODY = 4


def _params(n_axes):
    return pltpu.CompilerParams(dimension_semantics=("arbitrary",) * n_axes,
                                vmem_limit_bytes=VMEM_LIMIT)


def _const_spec(shape):
    nd = len(shape)
    return pl.BlockSpec(shape, lambda *_: (0,) * nd, pipeline_mode=pl.Buffered(1))


def _sigmoid(v):
    return jax.nn.sigmoid(v)


def _silu(v):
    return v * jax.nn.sigmoid(v)


def _bdot(a, b):
    return jnp.dot(a, b, preferred_element_type=jnp.float32)


def _zero_from(v):
    bits = pltpu.bitcast(v, jnp.uint32)
    return pltpu.bitcast((bits >> 16) >> 16, jnp.float32)


def _residue_perm(dilation):
    chunk = SUB_ROWS * dilation
    p = np.zeros((PERM_ROWS, PERM_ROWS), np.float32)
    for a in range(PERM_ROWS // chunk):
        for r in range(dilation):
            for j in range(SUB_ROWS):
                p[a * chunk + r * SUB_ROWS + j, a * chunk + j * dilation + r] = 1.0
    return p


def _mod_kernel(c_ref, w_ref, b_ref, o_ref):
    sc = _silu(c_ref[...])
    o_ref[...] = jnp.dot(sc, w_ref[...], preferred_element_type=jnp.float32,
                         precision=lax.Precision.HIGHEST) + b_ref[...]


def _mod_call(c, w_ada, b_ada):
    bsz = c.shape[0]
    n_out = w_ada.shape[1]
    bn = 1024
    return pl.pallas_call(
        _mod_kernel,
        grid=(n_out // bn,),
        in_specs=[pl.BlockSpec((bsz, D_MODEL), lambda j: (0, 0)),
                  pl.BlockSpec((D_MODEL, bn), lambda j: (0, j)),
                  pl.BlockSpec((1, bn), lambda j: (0, j))],
        out_specs=pl.BlockSpec((bsz, bn), lambda j: (0, j)),
        out_shape=jax.ShapeDtypeStruct((bsz, n_out), jnp.float32),
        compiler_params=_params(1),
        name="mod",
    )(c, w_ada, b_ada.reshape(1, n_out))


def _normed_input(x, norm_w, scale, shift):
    ms = jnp.mean(x * x, axis=-1, keepdims=True)
    return x * lax.rsqrt(ms + EPS) * norm_w * (1.0 + scale) + shift


def _grouped_shape(bsz, seq, dilation, width):
    return (bsz, seq // (SUB_ROWS * dilation), dilation, SUB_ROWS, width)


def _grouped_tile_spec(tm, dilation, width):
    return pl.BlockSpec((1, tm // (SUB_ROWS * dilation), dilation, SUB_ROWS, width),
                        lambda b, s: (b, s, 0, 0, 0))


def _residue_major(val, dilation, buf_a, buf_b):
    tm = val.shape[0]
    if dilation == 1:
        return val
    buf_a[...] = val
    if dilation == GATHER_STRIDE:
        chunk = SUB_ROWS * dilation
        return jnp.concatenate(
            [buf_a[pl.ds(a * chunk + r, SUB_ROWS, stride=dilation), :]
             for a in range(tm // chunk) for r in range(dilation)], axis=0)
    assert dilation == GATHER_STRIDE * GATHER_STRIDE and PERM_ROWS == SUB_ROWS * dilation
    quarter = PERM_ROWS // GATHER_STRIDE
    for blk in range(tm // PERM_ROWS):
        for r0 in range(GATHER_STRIDE):
            buf_b[blk * PERM_ROWS + r0 * quarter:blk * PERM_ROWS + (r0 + 1) * quarter, :] = (
                buf_a[pl.ds(blk * PERM_ROWS + r0, quarter, stride=GATHER_STRIDE), :])
    return jnp.concatenate(
        [buf_b[pl.ds(blk * PERM_ROWS + (r % GATHER_STRIDE) * quarter + r // GATHER_STRIDE, SUB_ROWS,
                     stride=GATHER_STRIDE), :]
         for blk in range(tm // PERM_ROWS) for r in range(dilation)], axis=0)


def _qkv_kernel(x_ref, shift_ref, scale_ref, nw_ref, wq_ref, wk_ref, wv_ref, qnw_ref, knw_ref, pos_ref,
                freq_ref, *out_and_scratch):
    *out_refs, z_ref, buf_a_ref, buf_b_ref, h_ref = out_and_scratch
    w_refs = (wq_ref, wk_ref, wv_ref)
    x = x_ref[0]
    tm = x.shape[0]
    h_ref[...] = _normed_input(x, nw_ref[...], scale_ref[0], shift_ref[0]).astype(jnp.bfloat16)
    half_rows = tm // 2
    half = HEAD_DIM // 2
    pos = pos_ref[0]
    lane = lax.broadcasted_iota(jnp.int32, (half_rows, HEAD_DIM), 1)
    low = lane < half
    ang = jnp.where(low, pos[:half_rows], pos[half_rows:]) * freq_ref[...]
    cos_p, sin_p = jnp.cos(ang), jnp.sin(ang)
    cos_s, sin_s = pltpu.roll(cos_p, half, axis=1), pltpu.roll(sin_p, half, axis=1)
    cos_t = jnp.concatenate([jnp.where(low, cos_p, cos_s), jnp.where(low, cos_s, cos_p)], axis=0)
    sin_t = jnp.concatenate([jnp.where(low, -sin_p, sin_s), jnp.where(low, -sin_s, sin_p)], axis=0)
    norm_ws = (qnw_ref[...] * (LOG2E * HEAD_DIM ** -0.5), knw_ref[...])
    n_proj = 3 * N_GROUPS

    def project(idx):
        kind, g = divmod(idx, N_GROUPS)
        z_ref[idx] = _bdot(h_ref[...], w_refs[kind][:, g * GROUP_WIDTH:(g + 1) * GROUP_WIDTH])

    for idx in range(PROJ_AHEAD):
        project(idx)
    for idx in range(n_proj):
        kind, g = divmod(idx, N_GROUPS)
        dilation = DILATED_GROUPS[g][1]
        heads = []
        for j in range(HEADS_PER_GROUP):
            zh = z_ref[idx, :, j * HEAD_DIM:(j + 1) * HEAD_DIM]
            if kind < 2:
                ms = jnp.mean(zh * zh, axis=-1, keepdims=True)
                zn = zh * lax.rsqrt(ms + EPS) * norm_ws[kind]
                zh = zn * cos_t + pltpu.roll(zn, half, axis=1) * sin_t
            heads.append(_residue_major(zh, dilation, buf_a_ref.at[j], buf_b_ref.at[j]).astype(jnp.bfloat16))
        o_ref = out_refs[idx]
        o_ref[0] = jnp.concatenate(heads, axis=-1).reshape(o_ref.shape[1:])
        if idx + PROJ_AHEAD < n_proj:
            done = pltpu.bitcast(o_ref[0, 0, 0, :, 0:LANES], jnp.uint32)
            words = pltpu.bitcast(h_ref[0:SUB_ROWS, 0:LANES], jnp.uint32) + ((done >> 16) >> 16)
            h_ref[0:SUB_ROWS, 0:LANES] = pltpu.bitcast(words, jnp.bfloat16)
            project(idx + PROJ_AHEAD)


def _weight_cols_spec(offset, width):
    assert offset % width == 0
    return pl.BlockSpec((D_MODEL, width), lambda *_: (0, offset // width), pipeline_mode=pl.Buffered(1))


def _qkv_call(x, mod3, norm_w, w_in_bf, q_norm_w, k_norm_w, posf, freq2):
    bsz, seq, _ = x.shape
    tm = TM_QKV
    tok = lambda b, s: (b, s, 0)
    dils = [d for _, d in DILATED_GROUPS]
    qkv_width = N_GROUPS * GROUP_WIDTH
    return pl.pallas_call(
        _qkv_kernel,
        grid=(bsz, seq // tm),
        in_specs=[pl.BlockSpec((1, tm, D_MODEL), tok),
                  pl.BlockSpec((1, 1, D_MODEL), lambda b, s: (3 * b, 0, 0)),
                  pl.BlockSpec((1, 1, D_MODEL), lambda b, s: (3 * b + 1, 0, 0)),
                  _const_spec((1, D_MODEL)),
                  _weight_cols_spec(OFF_Q, qkv_width),
                  _weight_cols_spec(OFF_K, qkv_width),
                  _weight_cols_spec(OFF_V, qkv_width),
                  _const_spec((1, HEAD_DIM)),
                  _const_spec((1, HEAD_DIM)),
                  pl.BlockSpec((1, tm, HEAD_DIM), tok),
                  _const_spec((1, HEAD_DIM))],
        out_specs=[_grouped_tile_spec(tm, d, GROUP_WIDTH) for d in dils] * 3,
        out_shape=[jax.ShapeDtypeStruct(_grouped_shape(bsz, seq, d, GROUP_WIDTH), jnp.bfloat16)
                   for d in dils] * 3,
        scratch_shapes=[pltpu.VMEM((3 * N_GROUPS, tm, GROUP_WIDTH), jnp.float32),
                        pltpu.VMEM((HEADS_PER_GROUP, tm, HEAD_DIM), jnp.float32),
                        pltpu.VMEM((HEADS_PER_GROUP, tm, HEAD_DIM), jnp.float32),
                        pltpu.VMEM((tm, D_MODEL), jnp.bfloat16)],
        compiler_params=_params(2),
        name="qkv",
    )(x, mod3, mod3, norm_w.reshape(1, D_MODEL), w_in_bf, w_in_bf, w_in_bf,
      q_norm_w.reshape(1, HEAD_DIM), k_norm_w.reshape(1, HEAD_DIM), posf, freq2)


def _attn_kernel(q_ref, k_ref, v_ref, kp_ref, vp_ref, cap_ref, o_ref, l_ref, *, nb_step, res_step):
    no_prev = (pl.program_id(1) == 0).astype(jnp.int32)
    ones = jnp.ones((BLK, HEAD_DIM), jnp.bfloat16)
    lane = lax.broadcasted_iota(jnp.int32, (BLK, HEAD_DIM), 1)
    nt_dims = (((1,), (1,)), ((), ()))

    def tile(ref, chunk0, r, j):
        t = ref[0, pl.ds(chunk0, CHUNKS_PER_BLK), r, :, j * HEAD_DIM:(j + 1) * HEAD_DIM]
        return t.reshape(BLK, HEAD_DIM)

    def process(pairs, first):
        units = [(i, r, j) for i, r in pairs for j in range(HEADS_PER_GROUP)]
        cap = cap_ref[no_prev] if first else cap_ref[0]
        q = [tile(q_ref, i * CHUNKS_PER_BLK, r, j) for i, r, j in units]
        kc = [tile(k_ref, i * CHUNKS_PER_BLK, r, j) for i, r, j in units]
        vc = [tile(v_ref, i * CHUNKS_PER_BLK, r, j) for i, r, j in units]
        if first:
            kp = [tile(kp_ref, 0, r, j) for _, r, j in units]
            vp = [tile(vp_ref, 0, r, j) for _, r, j in units]
        else:
            kp = [tile(k_ref, (i - 1) * CHUNKS_PER_BLK, r, j) for i, r, j in units]
            vp = [tile(v_ref, (i - 1) * CHUNKS_PER_BLK, r, j) for i, r, j in units]
        s = [lax.dot_general(a, jnp.concatenate([b, c], axis=0), nt_dims, preferred_element_type=jnp.float32)
             for a, b, c in zip(q, kp, kc)]
        s = [jnp.minimum(t, cap) for t in s]
        m = [jnp.max(t, axis=-1, keepdims=True) for t in s]
        p = [jnp.exp2(t - mm).astype(jnp.bfloat16) for t, mm in zip(s, m)]
        oe = [_bdot(t, jnp.concatenate([jnp.concatenate([va, ones], axis=-1),
                                        jnp.concatenate([vb, ones], axis=-1)], axis=0))
              for t, va, vb in zip(p, vp, vc)]
        for n, (i, r) in enumerate(pairs):
            chunks = pl.ds(i * CHUNKS_PER_BLK, CHUNKS_PER_BLK)
            stats = jnp.zeros((BLK, HEAD_DIM), jnp.float32)
            for j in range(HEADS_PER_GROUP):
                u = n * HEADS_PER_GROUP + j
                o_ref[0, chunks, r, :, j * HEAD_DIM:(j + 1) * HEAD_DIM] = (
                    oe[u][:, :HEAD_DIM].astype(o_ref.dtype).reshape(CHUNKS_PER_BLK, SUB_ROWS, HEAD_DIM))
                stats = jnp.where(lane == j, m[u], stats)
                stats = jnp.where(lane == HEADS_PER_GROUP + j, oe[u][:, HEAD_DIM:], stats)
            l_ref[0, chunks, r, :, :] = stats.reshape(CHUNKS_PER_BLK, SUB_ROWS, HEAD_DIM)

    def run(n_pairs, pair_of, first):
        left = n_pairs % PAIRS_PER_BODY
        if left:
            process([pair_of(t) for t in range(left)], first)
        if n_pairs >= PAIRS_PER_BODY:
            def body(t, carry):
                t0 = left + PAIRS_PER_BODY * t
                process([pair_of(t0 + n) for n in range(PAIRS_PER_BODY)], first)
                return carry
            lax.fori_loop(0, n_pairs // PAIRS_PER_BODY, body, 0)

    shift = res_step.bit_length() - 1

    def later_pair(t):
        return 1 + (t >> shift), t & (res_step - 1)

    run(res_step, lambda t: (0, t), True)
    run((nb_step - 1) * res_step, later_pair, False)


def _attn_call(q, k, v, caps, dilation):
    bsz, n_chunks = q.shape[:2]
    nb = n_chunks // CHUNKS_PER_BLK
    nb_step = min(nb, ATTN_PAIRS)
    res_step = min(ATTN_PAIRS // nb_step, dilation)
    assert res_step & (res_step - 1) == 0 and nb % nb_step == 0 and dilation % res_step == 0

    def cur(width):
        return pl.BlockSpec((1, nb_step * CHUNKS_PER_BLK, res_step, SUB_ROWS, width),
                            lambda b, n, rc: (b, n, rc, 0, 0))
    prev = pl.BlockSpec((1, CHUNKS_PER_BLK, res_step, SUB_ROWS, GROUP_WIDTH),
                        lambda b, n, rc: (b, jnp.maximum(n * nb_step - 1, 0), rc, 0, 0))
    qkv_spec = cur(GROUP_WIDTH)
    return pl.pallas_call(
        functools.partial(_attn_kernel, nb_step=nb_step, res_step=res_step),
        grid=(bsz, nb // nb_step, dilation // res_step),
        in_specs=[qkv_spec, qkv_spec, qkv_spec, prev, prev, _const_spec(caps.shape)],
        out_specs=[qkv_spec, cur(HEAD_DIM)],
        out_shape=[jax.ShapeDtypeStruct(q.shape, jnp.bfloat16),
                   jax.ShapeDtypeStruct(q.shape[:-1] + (HEAD_DIM,), jnp.float32)],
        compiler_params=_params(3),
        name=f"attn_d{dilation}",
    )(q, k, v, k, v, caps)


def _tail_kernel(x_ref, shift_ref, scale_ref, gate_ref, nw_ref,
                 o0_ref, o1_ref, o2_ref, l0_ref, l1_ref, l2_ref, pinv_ref, pinv32_ref,
                 wa_ref, wb_ref, wgc_ref, wga_ref, wmc_ref, wma_ref,
                 cw_ref, cb_ref, lnw_ref, lnb_ref, wco_ref, wao_ref, wout_ref,
                 out_ref, useq_ref, cseq_ref, side_ref, otok_ref, stok_ref, tc_ref):
    tm = x_ref.shape[1]
    nsl = D_MODEL // LANES

    @pl.when(pl.program_id(1) == 0)
    def _():
        useq_ref[0:HALO * nsl, :] = jnp.zeros((HALO * nsl, LANES), jnp.float32)

    x = x_ref[0]
    h = _normed_input(x, nw_ref[...], scale_ref[0], shift_ref[0]).astype(jnp.bfloat16)

    for c0 in range(0, D_MODEL, SIDE_PIECE):
        u = _bdot(h, wa_ref[:, c0:c0 + SIDE_PIECE]) * _sigmoid(_bdot(h, wb_ref[:, c0:c0 + SIDE_PIECE]))
        for s in range(SIDE_PIECE // LANES):
            row0 = HALO * nsl + c0 // LANES + s
            useq_ref[pl.ds(row0, tm, stride=nsl), :] = u[:, s * LANES:(s + 1) * LANES]
    def side_piece(w_ref, c0, dst):
        side_ref[:, dst:dst + SIDE_PIECE] = _bdot(h, w_ref[:, c0:c0 + SIDE_PIECE])
        return lambda: side_ref[0:nsl, dst:dst + LANES]

    def token_order_job(src_ref, dst_ref, g, lo, pmat_ref, **kw):
        rows = src_ref[0].reshape(tm, src_ref.shape[-1])[lo:lo + PERM_ROWS]
        dst_ref[g - 1, lo:lo + PERM_ROWS, :] = jnp.dot(pmat_ref[g - 1], rows,
                                                        preferred_element_type=jnp.float32, **kw)
        return lambda: dst_ref[g - 1, lo:lo + nsl, 0:LANES]

    early_jobs, late_jobs, col = [], [], 0
    for w_ref, early in ((wgc_ref, True), (wga_ref, True), (wmc_ref, False), (wma_ref, False)):
        for c0 in range(0, w_ref.shape[1], SIDE_PIECE):
            (early_jobs if early else late_jobs).append(
                functools.partial(side_piece, w_ref, c0, col + c0))
        col += w_ref.shape[1]
    for g, (o_ref, l_ref) in enumerate(((o1_ref, l1_ref), (o2_ref, l2_ref)), start=1):
        for lo in range(0, tm, PERM_ROWS):
            early_jobs.append(functools.partial(token_order_job, o_ref, otok_ref, g, lo, pinv_ref))
            early_jobs.append(functools.partial(token_order_job, l_ref, stok_ref, g, lo, pinv32_ref,
                                                precision=lax.Precision.HIGHEST))
    n_chunks = tm // CONV_CHUNK
    parked = {}
    for c in range(n_chunks):
        base = c * CONV_CHUNK * nsl
        bias = cb_ref[...]
        for read_back in parked.get(c - SIDE_LAG, ()):
            bias = bias + _zero_from(read_back())
        acc = jnp.broadcast_to(bias[None], (CONV_CHUNK, nsl, LANES))
        for t in range(CONV_KERNEL):
            start = base + (HALO - (CONV_KERNEL - 1) + t) * nsl
            taps = useq_ref[pl.ds(start, CONV_CHUNK * nsl), :].reshape(CONV_CHUNK, nsl, LANES)
            acc = acc + cw_ref[t][None] * taps
        cseq_ref[pl.ds(base, CONV_CHUNK * nsl), :] = acc.reshape(CONV_CHUNK * nsl, LANES)
        lo, hi = len(early_jobs) * c // n_chunks, len(early_jobs) * (c + 1) // n_chunks
        parked[c] = [job() for job in early_jobs[lo:hi]]
    late_reads = [job() for job in late_jobs]
    z_gc = side_ref[:, 0:D_MODEL]
    z_ga = side_ref[:, D_MODEL:D_MODEL + GROUP_WIDTH]
    z_mc = side_ref[:, D_MODEL + GROUP_WIDTH:2 * D_MODEL + GROUP_WIDTH]
    z_ma = side_ref[:, 2 * D_MODEL + GROUP_WIDTH:3 * D_MODEL + GROUP_WIDTH]
    useq_ref[0:HALO * nsl, :] = useq_ref[tm * nsl:(tm + HALO) * nsl, :]
    acc = jnp.concatenate([cseq_ref[pl.ds(s, tm, stride=nsl), :] for s in range(nsl)], axis=-1)
    mu = jnp.mean(acc, axis=-1, keepdims=True)
    cen = acc - mu
    var = jnp.mean(cen * cen, axis=-1, keepdims=True)
    ln = cen * lax.rsqrt(var + EPS) * lnw_ref[...] + lnb_ref[...]
    tc_ref[...] = (_silu(ln) * _silu(z_gc)).astype(jnp.bfloat16)
    words = pltpu.bitcast(tc_ref[0:SUB_ROWS, 0:LANES], jnp.uint32)
    for read_back in late_reads:
        words = words + ((pltpu.bitcast(read_back(), jnp.uint32) >> 16) >> 16)
    tc_ref[0:SUB_ROWS, 0:LANES] = pltpu.bitcast(words, jnp.bfloat16)
    y_conv = _bdot(tc_ref[...], wco_ref[...])

    o_g = [o0_ref[0].reshape(tm, GROUP_WIDTH).astype(jnp.float32), otok_ref[0], otok_ref[1]]
    stats = [l0_ref[0].reshape(tm, HEAD_DIM), stok_ref[0], stok_ref[1]]
    top = jnp.maximum(jnp.maximum(stats[0], stats[1]), stats[2])
    e_g = [jnp.exp2(st - top) for st in stats]
    dens = [pltpu.roll(st, LANES - HEADS_PER_GROUP, axis=1) for st in stats]
    total = e_g[0] * dens[0] + e_g[1] * dens[1] + e_g[2] * dens[2]
    head_lane = lax.broadcasted_iota(jnp.int32, total.shape, 1) < HEADS_PER_GROUP
    inv = 1.0 / jnp.where(head_lane, total, 1.0)
    w_g = [e * inv for e in e_g]
    heads = []
    for j in range(HEADS_PER_GROUP):
        cols = slice(j * HEAD_DIM, (j + 1) * HEAD_DIM)
        heads.append(sum(w[:, j:j + 1] * o[:, cols] for w, o in zip(w_g, o_g)))
    o = jnp.concatenate(heads, axis=-1)
    ta = (o * _silu(z_ga)).astype(jnp.bfloat16)
    y_attn = _bdot(ta, wao_ref[...])

    y = _sigmoid(z_mc) * y_conv + _sigmoid(z_ma) * y_attn
    out = _bdot(y.astype(jnp.bfloat16), wout_ref[...])
    out_ref[0] = x + gate_ref[0] * out


def _tail_call(x, mod3, norm_w, o_groups, l_groups, pinv, pinv32, w_in_bf, conv_w, conv_b, ln_w, ln_b,
               w_co, w_ao, w_out):
    side_cols = ((OFF_A, D_MODEL), (OFF_B, D_MODEL), (OFF_GC, D_MODEL),
                 (OFF_GA, GROUP_WIDTH), (OFF_MC, D_MODEL), (OFF_MA, D_MODEL))
    bsz, seq, _ = x.shape
    tm = TM_TAIL
    nsl = D_MODEL // LANES
    tok = lambda b, s: (b, s, 0)
    row = lambda v: v.reshape(1, D_MODEL)
    modspec = lambda k: pl.BlockSpec((1, 1, D_MODEL), lambda b, s: (3 * b + k, 0, 0))
    dils = [d for _, d in DILATED_GROUPS]
    in_specs = ([pl.BlockSpec((1, tm, D_MODEL), tok), modspec(0), modspec(1), modspec(2),
                 _const_spec((1, D_MODEL))]
                + [_grouped_tile_spec(tm, d, GROUP_WIDTH) for d in dils]
                + [_grouped_tile_spec(tm, d, HEAD_DIM) for d in dils]
                + [_const_spec(pinv.shape), _const_spec(pinv32.shape)]
                + [_weight_cols_spec(off, width) for off, width in side_cols]
                + [_const_spec((CONV_KERNEL, nsl, LANES)), _const_spec((nsl, LANES))]
                + [_const_spec((1, D_MODEL))] * 2
                + [_const_spec(w_co.shape), _const_spec(w_ao.shape), _const_spec(w_out.shape)])
    return pl.pallas_call(
        _tail_kernel,
        grid=(bsz, seq // tm),
        in_specs=in_specs,
        out_specs=pl.BlockSpec((1, tm, D_MODEL), tok),
        out_shape=jax.ShapeDtypeStruct(x.shape, x.dtype),
        scratch_shapes=[pltpu.VMEM(((tm + HALO) * nsl, LANES), jnp.float32),
                        pltpu.VMEM((tm * nsl, LANES), jnp.float32),
                        pltpu.VMEM((tm, 3 * D_MODEL + GROUP_WIDTH), jnp.float32),
                        pltpu.VMEM((N_GROUPS - 1, tm, GROUP_WIDTH), jnp.float32),
                        pltpu.VMEM((N_GROUPS - 1, tm, HEAD_DIM), jnp.float32),
                        pltpu.VMEM((tm, D_MODEL), jnp.bfloat16)],
        compiler_params=_params(2),
        name="tail",
    )(x, mod3, mod3, mod3, row(norm_w), *o_groups, *l_groups, pinv, pinv32, *([w_in_bf] * len(side_cols)),
      conv_w.reshape(CONV_KERNEL, nsl, LANES), conv_b.reshape(nsl, LANES), row(ln_w), row(ln_b),
      w_co, w_ao, w_out)


def _attention_caps():
    qi = np.arange(BLK)[:, None]
    kj = np.arange(BLK)[None, :]
    prev = np.where(kj >= qi, BIG, NEG_INF)
    cur = np.where(kj <= qi, BIG, NEG_INF)
    no_prev = np.full((BLK, BLK), NEG_INF)
    caps = np.stack([np.concatenate([prev, cur], axis=1),
                     np.concatenate([no_prev, cur], axis=1)]).astype(np.float32)
    return jnp.asarray(caps)


def kernel(x, c, positions, norm_w, w_ada, b_ada, w_in, conv_w, conv_b, conv_ln_w, conv_ln_b,
           w_conv_out, q_norm_w, k_norm_w, w_attn_out, w_out):
    bsz, seq, _ = x.shape
    bf = jnp.bfloat16
    for window, dilation in DILATED_GROUPS:
        assert window // dilation == BLK and seq % (dilation * BLK) == 0
        assert PERM_ROWS % (SUB_ROWS * dilation) == 0
    assert seq % TM_QKV == 0 and seq % TM_TAIL == 0
    assert TM_QKV % PERM_ROWS == 0 and TM_TAIL % PERM_ROWS == 0

    mod3 = _mod_call(c, w_ada, b_ada).reshape(bsz * 3, 1, D_MODEL)

    w_in_bf = w_in.astype(bf)
    inv_freq = ROPE_THETA ** (-jnp.arange(0, HEAD_DIM, 2, dtype=jnp.float32) / HEAD_DIM)
    freq2 = jnp.concatenate([inv_freq, inv_freq]).reshape(1, HEAD_DIM)
    posf = jnp.broadcast_to(positions.astype(jnp.float32)[..., None], (bsz, seq, HEAD_DIM))
    qkv = _qkv_call(x, mod3, norm_w, w_in_bf, q_norm_w, k_norm_w, posf, freq2)

    caps = _attention_caps()
    o_groups, l_groups = [], []
    for g, (_, dilation) in enumerate(DILATED_GROUPS):
        o_g, l_g = _attn_call(qkv[g], qkv[N_GROUPS + g], qkv[2 * N_GROUPS + g], caps, dilation)
        o_groups.append(o_g)
        l_groups.append(l_g)

    pinv32 = jnp.asarray(np.stack([_residue_perm(d).T for _, d in DILATED_GROUPS[1:]]))
    return _tail_call(x, mod3, norm_w, o_groups, l_groups, pinv32.astype(bf), pinv32, w_in_bf,
                      conv_w, conv_b, conv_ln_w, conv_ln_b,
                      w_conv_out.astype(bf), w_attn_out.astype(bf), w_out.astype(bf))
```

```python
import functools
import math

import jax
import jax.numpy as jnp
import numpy as np
from jax import lax
from jax.experimental import pallas as pl
from jax.experimental.pallas import tpu as pltpu

D_MODEL = 1024
CONV_KERNEL = 31
HEAD_DIM = 128
HEADS_PER_GROUP = 4
GROUP_WIDTH = HEADS_PER_GROUP * HEAD_DIM
DILATED_GROUPS = ((128, 1), (512, 4), (2048, 16))
N_GROUPS = len(DILATED_GROUPS)
BLK = 128
ROPE_THETA = 10000.0
EPS = 1e-6
NEG_INF = -1e30
BIG = 3.0e38
LOG2E = math.log2(math.e)

OFF_A, OFF_B, OFF_GC = 0, 1024, 2048
OFF_Q, OFF_K, OFF_V = 3072, 4608, 6144
OFF_GA, OFF_MC, OFF_MA = 7680, 8192, 9216

LANES = 128
SUB_ROWS = 16
CHUNKS_PER_BLK = BLK // SUB_ROWS
PERM_ROWS = 256
GATHER_STRIDE = 4
HALO = 32
CONV_CHUNK = 8
SIDE_PIECE = 256
SIDE_LAG = 8
VMEM_LIMIT = 56 * 1024 * 1024

TM_QKV = 512
TM_TAIL = 512
PROJ_AHEAD = 1
ATTN_PAIRS = 16
PAIRS_PER_BODY = 4


def _params(n_axes):
    return pltpu.CompilerParams(dimension_semantics=("arbitrary",) * n_axes,
                                vmem_limit_bytes=VMEM_LIMIT)


def _const_spec(shape):
    nd = len(shape)
    return pl.BlockSpec(shape, lambda *_: (0,) * nd, pipeline_mode=pl.Buffered(1))


def _sigmoid(v):
    return jax.nn.sigmoid(v)


def _silu(v):
    return v * jax.nn.sigmoid(v)


def _bdot(a, b):
    return jnp.dot(a, b, preferred_element_type=jnp.float32)


def _zero_from(v):
    bits = pltpu.bitcast(v, jnp.uint32)
    return pltpu.bitcast((bits >> 16) >> 16, jnp.float32)


def _residue_perm(dilation):
    chunk = SUB_ROWS * dilation
    p = np.zeros((PERM_ROWS, PERM_ROWS), np.float32)
    for a in range(PERM_ROWS // chunk):
        for r in range(dilation):
            for j in range(SUB_ROWS):
                p[a * chunk + r * SUB_ROWS + j, a * chunk + j * dilation + r] = 1.0
    return p


def _mod_kernel(c_ref, w_ref, b_ref, o_ref):
    sc = _silu(c_ref[...])
    o_ref[...] = jnp.dot(sc, w_ref[...], preferred_element_type=jnp.float32,
                         precision=lax.Precision.HIGHEST) + b_ref[...]


def _mod_call(c, w_ada, b_ada):
    bsz = c.shape[0]
    n_out = w_ada.shape[1]
    bn = 1024
    return pl.pallas_call(
        _mod_kernel,
        grid=(n_out // bn,),
        in_specs=[pl.BlockSpec((bsz, D_MODEL), lambda j: (0, 0)),
                  pl.BlockSpec((D_MODEL, bn), lambda j: (0, j)),
                  pl.BlockSpec((1, bn), lambda j: (0, j))],
        out_specs=pl.BlockSpec((bsz, bn), lambda j: (0, j)),
        out_shape=jax.ShapeDtypeStruct((bsz, n_out), jnp.float32),
        compiler_params=_params(1),
        name="mod",
    )(c, w_ada, b_ada.reshape(1, n_out))


def _normed_input(x, norm_w, scale, shift):
    ms = jnp.mean(x * x, axis=-1, keepdims=True)
    return x * lax.rsqrt(ms + EPS) * norm_w * (1.0 + scale) + shift


def _grouped_shape(bsz, seq, dilation, width):
    return (bsz, seq // (SUB_ROWS * dilation), dilation, SUB_ROWS, width)


def _grouped_tile_spec(tm, dilation, width):
    return pl.BlockSpec((1, tm // (SUB_ROWS * dilation), dilation, SUB_ROWS, width),
                        lambda b, s: (b, s, 0, 0, 0))


def _residue_major(val, dilation, buf_a, buf_b):
    tm = val.shape[0]
    if dilation == 1:
        return val
    buf_a[...] = val
    if dilation == GATHER_STRIDE:
        chunk = SUB_ROWS * dilation
        return jnp.concatenate(
            [buf_a[pl.ds(a * chunk + r, SUB_ROWS, stride=dilation), :]
             for a in range(tm // chunk) for r in range(dilation)], axis=0)
    assert dilation == GATHER_STRIDE * GATHER_STRIDE and PERM_ROWS == SUB_ROWS * dilation
    quarter = PERM_ROWS // GATHER_STRIDE
    for blk in range(tm // PERM_ROWS):
        for r0 in range(GATHER_STRIDE):
            buf_b[blk * PERM_ROWS + r0 * quarter:blk * PERM_ROWS + (r0 + 1) * quarter, :] = (
                buf_a[pl.ds(blk * PERM_ROWS + r0, quarter, stride=GATHER_STRIDE), :])
    return jnp.concatenate(
        [buf_b[pl.ds(blk * PERM_ROWS + (r % GATHER_STRIDE) * quarter + r // GATHER_STRIDE, SUB_ROWS,
                     stride=GATHER_STRIDE), :]
         for blk in range(tm // PERM_ROWS) for r in range(dilation)], axis=0)


def _qkv_kernel(x_ref, shift_ref, scale_ref, nw_ref, wq_ref, wk_ref, wv_ref, qnw_ref, knw_ref, pos_ref,
                freq_ref, *out_and_scratch):
    *out_refs, z_ref, buf_a_ref, buf_b_ref, h_ref = out_and_scratch
    w_refs = (wq_ref, wk_ref, wv_ref)
    x = x_ref[0]
    tm = x.shape[0]
    h_ref[...] = _normed_input(x, nw_ref[...], scale_ref[0], shift_ref[0]).astype(jnp.bfloat16)
    half_rows = tm // 2
    half = HEAD_DIM // 2
    pos = pos_ref[0]
    lane = lax.broadcasted_iota(jnp.int32, (half_rows, HEAD_DIM), 1)
    low = lane < half
    ang = jnp.where(low, pos[:half_rows], pos[half_rows:]) * freq_ref[...]
    cos_p, sin_p = jnp.cos(ang), jnp.sin(ang)
    cos_s, sin_s = pltpu.roll(cos_p, half, axis=1), pltpu.roll(sin_p, half, axis=1)
    cos_t = jnp.concatenate([jnp.where(low, cos_p, cos_s), jnp.where(low, cos_s, cos_p)], axis=0)
    sin_t = jnp.concatenate([jnp.where(low, -sin_p, sin_s), jnp.where(low, -sin_s, sin_p)], axis=0)
    norm_ws = (qnw_ref[...] * (LOG2E * HEAD_DIM ** -0.5), knw_ref[...])
    n_proj = 3 * N_GROUPS

    def project(idx):
        kind, g = divmod(idx, N_GROUPS)
        z_ref[idx] = _bdot(h_ref[...], w_refs[kind][:, g * GROUP_WIDTH:(g + 1) * GROUP_WIDTH])

    for idx in range(PROJ_AHEAD):
        project(idx)
    for idx in range(n_proj):
        kind, g = divmod(idx, N_GROUPS)
        dilation = DILATED_GROUPS[g][1]
        heads = []
        for j in range(HEADS_PER_GROUP):
            zh = z_ref[idx, :, j * HEAD_DIM:(j + 1) * HEAD_DIM]
            if kind < 2:
                ms = jnp.mean(zh * zh, axis=-1, keepdims=True)
                zn = zh * lax.rsqrt(ms + EPS) * norm_ws[kind]
                zh = zn * cos_t + pltpu.roll(zn, half, axis=1) * sin_t
            heads.append(_residue_major(zh, dilation, buf_a_ref.at[j], buf_b_ref.at[j]).astype(jnp.bfloat16))
        o_ref = out_refs[g]
        c0 = kind * GROUP_WIDTH
        o_ref[0, :, :, :, c0:c0 + GROUP_WIDTH] = (
            jnp.concatenate(heads, axis=-1).reshape(o_ref.shape[1:-1] + (GROUP_WIDTH,)))
        if idx + PROJ_AHEAD < n_proj:
            done = pltpu.bitcast(o_ref[0, 0, 0, :, c0:c0 + LANES], jnp.uint32)
            words = pltpu.bitcast(h_ref[0:SUB_ROWS, 0:LANES], jnp.uint32) + ((done >> 16) >> 16)
            h_ref[0:SUB_ROWS, 0:LANES] = pltpu.bitcast(words, jnp.bfloat16)
            project(idx + PROJ_AHEAD)


def _weight_cols_spec(offset, width):
    assert offset % width == 0
    return pl.BlockSpec((D_MODEL, width), lambda *_: (0, offset // width), pipeline_mode=pl.Buffered(1))


def _qkv_call(x, mod3, norm_w, w_in_bf, q_norm_w, k_norm_w, posf, freq2):
    bsz, seq, _ = x.shape
    tm = TM_QKV
    tok = lambda b, s: (b, s, 0)
    dils = [d for _, d in DILATED_GROUPS]
    qkv_width = N_GROUPS * GROUP_WIDTH
    return pl.pallas_call(
        _qkv_kernel,
        grid=(bsz, seq // tm),
        in_specs=[pl.BlockSpec((1, tm, D_MODEL), tok),
                  pl.BlockSpec((1, 1, D_MODEL), lambda b, s: (3 * b, 0, 0)),
                  pl.BlockSpec((1, 1, D_MODEL), lambda b, s: (3 * b + 1, 0, 0)),
                  _const_spec((1, D_MODEL)),
                  _weight_cols_spec(OFF_Q, qkv_width),
                  _weight_cols_spec(OFF_K, qkv_width),
                  _weight_cols_spec(OFF_V, qkv_width),
                  _const_spec((1, HEAD_DIM)),
                  _const_spec((1, HEAD_DIM)),
                  pl.BlockSpec((1, tm, HEAD_DIM), tok),
                  _const_spec((1, HEAD_DIM))],
        out_specs=[_grouped_tile_spec(tm, d, qkv_width) for d in dils],
        out_shape=[jax.ShapeDtypeStruct(_grouped_shape(bsz, seq, d, qkv_width), jnp.bfloat16)
                   for d in dils],
        scratch_shapes=[pltpu.VMEM((3 * N_GROUPS, tm, GROUP_WIDTH), jnp.float32),
                        pltpu.VMEM((HEADS_PER_GROUP, tm, HEAD_DIM), jnp.float32),
                        pltpu.VMEM((HEADS_PER_GROUP, tm, HEAD_DIM), jnp.float32),
                        pltpu.VMEM((tm, D_MODEL), jnp.bfloat16)],
        compiler_params=_params(2),
        name="qkv",
    )(x, mod3, mod3, norm_w.reshape(1, D_MODEL), w_in_bf, w_in_bf, w_in_bf,
      q_norm_w.reshape(1, HEAD_DIM), k_norm_w.reshape(1, HEAD_DIM), posf, freq2)


def _attn_kernel(q_ref, k_ref, v_ref, kp_ref, vp_ref, cap_ref, o_ref, l_ref, *, nb_step, res_step):
    no_prev = (pl.program_id(1) == 0).astype(jnp.int32)
    ones = jnp.ones((BLK, HEAD_DIM), jnp.bfloat16)
    lane = lax.broadcasted_iota(jnp.int32, (BLK, HEAD_DIM), 1)
    nt_dims = (((1,), (1,)), ((), ()))

    def tile(ref, chunk0, r, j):
        t = ref[0, pl.ds(chunk0, CHUNKS_PER_BLK), r, :, j * HEAD_DIM:(j + 1) * HEAD_DIM]
        return t.reshape(BLK, HEAD_DIM)

    def process(pairs, first):
        units = [(i, r, j) for i, r in pairs for j in range(HEADS_PER_GROUP)]
        cap = cap_ref[no_prev] if first else cap_ref[0]
        q = [tile(q_ref, i * CHUNKS_PER_BLK, r, j) for i, r, j in units]
        kc = [tile(k_ref, i * CHUNKS_PER_BLK, r, j) for i, r, j in units]
        vc = [tile(v_ref, i * CHUNKS_PER_BLK, r, j) for i, r, j in units]
        if first:
            kp = [tile(kp_ref, 0, r, j) for _, r, j in units]
            vp = [tile(vp_ref, 0, r, j) for _, r, j in units]
        else:
            kp = [tile(k_ref, (i - 1) * CHUNKS_PER_BLK, r, j) for i, r, j in units]
            vp = [tile(v_ref, (i - 1) * CHUNKS_PER_BLK, r, j) for i, r, j in units]
        s = [lax.dot_general(a, jnp.concatenate([b, c], axis=0), nt_dims, preferred_element_type=jnp.float32)
             for a, b, c in zip(q, kp, kc)]
        s = [jnp.minimum(t, cap) for t in s]
        m = [jnp.max(t, axis=-1, keepdims=True) for t in s]
        p = [jnp.exp2(t - mm).astype(jnp.bfloat16) for t, mm in zip(s, m)]
        oe = [_bdot(t, jnp.concatenate([jnp.concatenate([va, ones], axis=-1),
                                        jnp.concatenate([vb, ones], axis=-1)], axis=0))
              for t, va, vb in zip(p, vp, vc)]
        for n, (i, r) in enumerate(pairs):
            chunks = pl.ds(i * CHUNKS_PER_BLK, CHUNKS_PER_BLK)
            stats = jnp.zeros((BLK, HEAD_DIM), jnp.float32)
            for j in range(HEADS_PER_GROUP):
                u = n * HEADS_PER_GROUP + j
                o_ref[0, chunks, r, :, j * HEAD_DIM:(j + 1) * HEAD_DIM] = (
                    oe[u][:, :HEAD_DIM].astype(o_ref.dtype).reshape(CHUNKS_PER_BLK, SUB_ROWS, HEAD_DIM))
                stats = jnp.where(lane == j, m[u], stats)
                stats = jnp.where(lane == HEADS_PER_GROUP + j, oe[u][:, HEAD_DIM:], stats)
            l_ref[0, chunks, r, :, :] = stats.reshape(CHUNKS_PER_BLK, SUB_ROWS, HEAD_DIM)

    def run(n_pairs, pair_of, first):
        left = n_pairs % PAIRS_PER_BODY
        if left:
            process([pair_of(t) for t in range(left)], first)
        if n_pairs >= PAIRS_PER_BODY:
            def body(t, carry):
                t0 = left + PAIRS_PER_BODY * t
                process([pair_of(t0 + n) for n in range(PAIRS_PER_BODY)], first)
                return carry
            lax.fori_loop(0, n_pairs // PAIRS_PER_BODY, body, 0)

    shift = res_step.bit_length() - 1

    def later_pair(t):
        return 1 + (t >> shift), t & (res_step - 1)

    run(res_step, lambda t: (0, t), True)
    run((nb_step - 1) * res_step, later_pair, False)


def _attn_call(qkv, caps, dilation):
    bsz, n_chunks = qkv.shape[:2]
    nb = n_chunks // CHUNKS_PER_BLK
    nb_step = min(nb, ATTN_PAIRS)
    res_step = min(ATTN_PAIRS // nb_step, dilation)
    assert res_step & (res_step - 1) == 0 and nb % nb_step == 0 and dilation % res_step == 0

    def cur(width, col=0):
        return pl.BlockSpec((1, nb_step * CHUNKS_PER_BLK, res_step, SUB_ROWS, width),
                            lambda b, n, rc: (b, n, rc, 0, col))

    def prev(col):
        return pl.BlockSpec((1, CHUNKS_PER_BLK, res_step, SUB_ROWS, GROUP_WIDTH),
                            lambda b, n, rc: (b, jnp.maximum(n * nb_step - 1, 0), rc, 0, col))

    group_shape = qkv.shape[:-1]
    return pl.pallas_call(
        functools.partial(_attn_kernel, nb_step=nb_step, res_step=res_step),
        grid=(bsz, nb // nb_step, dilation // res_step),
        in_specs=[cur(GROUP_WIDTH, 0), cur(GROUP_WIDTH, 1), cur(GROUP_WIDTH, 2), prev(1), prev(2),
                  _const_spec(caps.shape)],
        out_specs=[cur(GROUP_WIDTH), cur(HEAD_DIM)],
        out_shape=[jax.ShapeDtypeStruct(group_shape + (GROUP_WIDTH,), jnp.bfloat16),
                   jax.ShapeDtypeStruct(group_shape + (HEAD_DIM,), jnp.float32)],
        compiler_params=_params(3),
        name=f"attn_d{dilation}",
    )(qkv, qkv, qkv, qkv, qkv, caps)


def _tail_kernel(x_ref, shift_ref, scale_ref, gate_ref, nw_ref,
                 o0_ref, o1_ref, o2_ref, l0_ref, l1_ref, l2_ref, pinv_ref, pinv32_ref,
                 wa_ref, wb_ref, wgc_ref, wga_ref, wmc_ref, wma_ref,
                 cw_ref, cb_ref, lnw_ref, lnb_ref, wco_ref, wao_ref, wout_ref,
                 out_ref, useq_ref, cseq_ref, side_ref, otok_ref, stok_ref, tc_ref):
    tm = x_ref.shape[1]
    nsl = D_MODEL // LANES

    @pl.when(pl.program_id(1) == 0)
    def _():
        useq_ref[0:HALO * nsl, :] = jnp.zeros((HALO * nsl, LANES), jnp.float32)

    x = x_ref[0]
    h = _normed_input(x, nw_ref[...], scale_ref[0], shift_ref[0]).astype(jnp.bfloat16)

    for c0 in range(0, D_MODEL, SIDE_PIECE):
        u = _bdot(h, wa_ref[:, c0:c0 + SIDE_PIECE]) * _sigmoid(_bdot(h, wb_ref[:, c0:c0 + SIDE_PIECE]))
        for s in range(SIDE_PIECE // LANES):
            row0 = HALO * nsl + c0 // LANES + s
            useq_ref[pl.ds(row0, tm, stride=nsl), :] = u[:, s * LANES:(s + 1) * LANES]
    def side_piece(w_ref, c0, dst):
        side_ref[:, dst:dst + SIDE_PIECE] = _bdot(h, w_ref[:, c0:c0 + SIDE_PIECE])
        return lambda: side_ref[0:nsl, dst:dst + LANES]

    def token_order_job(src_ref, dst_ref, g, lo, pmat_ref, **kw):
        rows = src_ref[0].reshape(tm, src_ref.shape[-1])[lo:lo + PERM_ROWS]
        dst_ref[g - 1, lo:lo + PERM_ROWS, :] = jnp.dot(pmat_ref[g - 1], rows,
                                                        preferred_element_type=jnp.float32, **kw)
        return lambda: dst_ref[g - 1, lo:lo + nsl, 0:LANES]

    early_jobs, late_jobs, col = [], [], 0
    for w_ref, early in ((wgc_ref, True), (wga_ref, True), (wmc_ref, False), (wma_ref, False)):
        for c0 in range(0, w_ref.shape[1], SIDE_PIECE):
            (early_jobs if early else late_jobs).append(
                functools.partial(side_piece, w_ref, c0, col + c0))
        col += w_ref.shape[1]
    for g, (o_ref, l_ref) in enumerate(((o1_ref, l1_ref), (o2_ref, l2_ref)), start=1):
        for lo in range(0, tm, PERM_ROWS):
            early_jobs.append(functools.partial(token_order_job, o_ref, otok_ref, g, lo, pinv_ref))
            early_jobs.append(functools.partial(token_order_job, l_ref, stok_ref, g, lo, pinv32_ref,
                                                precision=lax.Precision.HIGHEST))
    n_chunks = tm // CONV_CHUNK
    parked = {}
    for c in range(n_chunks):
        base = c * CONV_CHUNK * nsl
        bias = cb_ref[...]
        for read_back in parked.get(c - SIDE_LAG, ()):
            bias = bias + _zero_from(read_back())
        acc = jnp.broadcast_to(bias[None], (CONV_CHUNK, nsl, LANES))
        for t in range(CONV_KERNEL):
            start = base + (HALO - (CONV_KERNEL - 1) + t) * nsl
            taps = useq_ref[pl.ds(start, CONV_CHUNK * nsl), :].reshape(CONV_CHUNK, nsl, LANES)
            acc = acc + cw_ref[t][None] * taps
        cseq_ref[pl.ds(base, CONV_CHUNK * nsl), :] = acc.reshape(CONV_CHUNK * nsl, LANES)
        lo, hi = len(early_jobs) * c // n_chunks, len(early_jobs) * (c + 1) // n_chunks
        parked[c] = [job() for job in early_jobs[lo:hi]]
    late_reads = [job() for job in late_jobs]
    z_gc = side_ref[:, 0:D_MODEL]
    z_ga = side_ref[:, D_MODEL:D_MODEL + GROUP_WIDTH]
    z_mc = side_ref[:, D_MODEL + GROUP_WIDTH:2 * D_MODEL + GROUP_WIDTH]
    z_ma = side_ref[:, 2 * D_MODEL + GROUP_WIDTH:3 * D_MODEL + GROUP_WIDTH]
    useq_ref[0:HALO * nsl, :] = useq_ref[tm * nsl:(tm + HALO) * nsl, :]
    acc = jnp.concatenate([cseq_ref[pl.ds(s, tm, stride=nsl), :] for s in range(nsl)], axis=-1)
    mu = jnp.mean(acc, axis=-1, keepdims=True)
    cen = acc - mu
    var = jnp.mean(cen * cen, axis=-1, keepdims=True)
    ln = cen * lax.rsqrt(var + EPS) * lnw_ref[...] + lnb_ref[...]
    tc_ref[...] = (_silu(ln) * _silu(z_gc)).astype(jnp.bfloat16)
    words = pltpu.bitcast(tc_ref[0:SUB_ROWS, 0:LANES], jnp.uint32)
    for read_back in late_reads:
        words = words + ((pltpu.bitcast(read_back(), jnp.uint32) >> 16) >> 16)
    tc_ref[0:SUB_ROWS, 0:LANES] = pltpu.bitcast(words, jnp.bfloat16)
    y_conv = _bdot(tc_ref[...], wco_ref[...])

    o_g = [o0_ref[0].reshape(tm, GROUP_WIDTH).astype(jnp.float32), otok_ref[0], otok_ref[1]]
    stats = [l0_ref[0].reshape(tm, HEAD_DIM), stok_ref[0], stok_ref[1]]
    top = jnp.maximum(jnp.maximum(stats[0], stats[1]), stats[2])
    e_g = [jnp.exp2(st - top) for st in stats]
    dens = [pltpu.roll(st, LANES - HEADS_PER_GROUP, axis=1) for st in stats]
    total = e_g[0] * dens[0] + e_g[1] * dens[1] + e_g[2] * dens[2]
    head_lane = lax.broadcasted_iota(jnp.int32, total.shape, 1) < HEADS_PER_GROUP
    inv = 1.0 / jnp.where(head_lane, total, 1.0)
    w_g = [e * inv for e in e_g]
    heads = []
    for j in range(HEADS_PER_GROUP):
        cols = slice(j * HEAD_DIM, (j + 1) * HEAD_DIM)
        heads.append(sum(w[:, j:j + 1] * o[:, cols] for w, o in zip(w_g, o_g)))
    o = jnp.concatenate(heads, axis=-1)
    ta = (o * _silu(z_ga)).astype(jnp.bfloat16)
    y_attn = _bdot(ta, wao_ref[...])

    y = _sigmoid(z_mc) * y_conv + _sigmoid(z_ma) * y_attn
    out = _bdot(y.astype(jnp.bfloat16), wout_ref[...])
    out_ref[0] = x + gate_ref[0] * out


def _tail_call(x, mod3, norm_w, o_groups, l_groups, pinv, pinv32, w_in_bf, conv_w, conv_b, ln_w, ln_b,
               w_co, w_ao, w_out):
    side_cols = ((OFF_A, D_MODEL), (OFF_B, D_MODEL), (OFF_GC, D_MODEL),
                 (OFF_GA, GROUP_WIDTH), (OFF_MC, D_MODEL), (OFF_MA, D_MODEL))
    bsz, seq, _ = x.shape
    tm = TM_TAIL
    nsl = D_MODEL // LANES
    tok = lambda b, s: (b, s, 0)
    row = lambda v: v.reshape(1, D_MODEL)
    modspec = lambda k: pl.BlockSpec((1, 1, D_MODEL), lambda b, s: (3 * b + k, 0, 0))
    dils = [d for _, d in DILATED_GROUPS]
    in_specs = ([pl.BlockSpec((1, tm, D_MODEL), tok), modspec(0), modspec(1), modspec(2),
                 _const_spec((1, D_MODEL))]
                + [_grouped_tile_spec(tm, d, GROUP_WIDTH) for d in dils]
                + [_grouped_tile_spec(tm, d, HEAD_DIM) for d in dils]
                + [_const_spec(pinv.shape), _const_spec(pinv32.shape)]
                + [_weight_cols_spec(off, width) for off, width in side_cols]
                + [_const_spec((CONV_KERNEL, nsl, LANES)), _const_spec((nsl, LANES))]
                + [_const_spec((1, D_MODEL))] * 2
                + [_const_spec(w_co.shape), _const_spec(w_ao.shape), _const_spec(w_out.shape)])
    return pl.pallas_call(
        _tail_kernel,
        grid=(bsz, seq // tm),
        in_specs=in_specs,
        out_specs=pl.BlockSpec((1, tm, D_MODEL), tok),
        out_shape=jax.ShapeDtypeStruct(x.shape, x.dtype),
        scratch_shapes=[pltpu.VMEM(((tm + HALO) * nsl, LANES), jnp.float32),
                        pltpu.VMEM((tm * nsl, LANES), jnp.float32),
                        pltpu.VMEM((tm, 3 * D_MODEL + GROUP_WIDTH), jnp.float32),
                        pltpu.VMEM((N_GROUPS - 1, tm, GROUP_WIDTH), jnp.float32),
                        pltpu.VMEM((N_GROUPS - 1, tm, HEAD_DIM), jnp.float32),
                        pltpu.VMEM((tm, D_MODEL), jnp.bfloat16)],
        compiler_params=_params(2),
        name="tail",
    )(x, mod3, mod3, mod3, row(norm_w), *o_groups, *l_groups, pinv, pinv32, *([w_in_bf] * len(side_cols)),
      conv_w.reshape(CONV_KERNEL, nsl, LANES), conv_b.reshape(nsl, LANES), row(ln_w), row(ln_b),
      w_co, w_ao, w_out)


def _attention_caps():
    qi = np.arange(BLK)[:, None]
    kj = np.arange(BLK)[None, :]
    prev = np.where(kj >= qi, BIG, NEG_INF)
    cur = np.where(kj <= qi, BIG, NEG_INF)
    no_prev = np.full((BLK, BLK), NEG_INF)
    caps = np.stack([np.concatenate([prev, cur], axis=1),
                     np.concatenate([no_prev, cur], axis=1)]).astype(np.float32)
    return jnp.asarray(caps)


def kernel(x, c, positions, norm_w, w_ada, b_ada, w_in, conv_w, conv_b, conv_ln_w, conv_ln_b,
           w_conv_out, q_norm_w, k_norm_w, w_attn_out, w_out):
    bsz, seq, _ = x.shape
    bf = jnp.bfloat16
    for window, dilation in DILATED_GROUPS:
        assert window // dilation == BLK and seq % (dilation * BLK) == 0
        assert PERM_ROWS % (SUB_ROWS * dilation) == 0
    assert seq % TM_QKV == 0 and seq % TM_TAIL == 0
    assert TM_QKV % PERM_ROWS == 0 and TM_TAIL % PERM_ROWS == 0

    mod3 = _mod_call(c, w_ada, b_ada).reshape(bsz * 3, 1, D_MODEL)

    w_in_bf = w_in.astype(bf)
    inv_freq = ROPE_THETA ** (-jnp.arange(0, HEAD_DIM, 2, dtype=jnp.float32) / HEAD_DIM)
    freq2 = jnp.concatenate([inv_freq, inv_freq]).reshape(1, HEAD_DIM)
    posf = jnp.broadcast_to(positions.astype(jnp.float32)[..., None], (bsz, seq, HEAD_DIM))
    qkv = _qkv_call(x, mod3, norm_w, w_in_bf, q_norm_w, k_norm_w, posf, freq2)

    caps = _attention_caps()
    o_groups, l_groups = [], []
    for g, (_, dilation) in enumerate(DILATED_GROUPS):
        o_g, l_g = _attn_call(qkv[g], caps, dilation)
        o_groups.append(o_g)
        l_groups.append(l_g)

    pinv32 = jnp.asarray(np.stack([_residue_perm(d).T for _, d in DILATED_GROUPS[1:]]))
    return _tail_call(x, mod3, norm_w, o_groups, l_groups, pinv32.astype(bf), pinv32, w_in_bf,
                      conv_w, conv_b, conv_ln_w, conv_ln_b,
                      w_conv_out.astype(bf), w_attn_out.astype(bf), w_out.astype(bf))
```

```python
import functools
import math

import jax
import jax.numpy as jnp
import numpy as np
from jax import lax
from jax.experimental import pallas as pl
from jax.experimental.pallas import tpu as pltpu

D_MODEL = 1024
CONV_KERNEL = 31
HEAD_DIM = 128
HEADS_PER_GROUP = 4
GROUP_WIDTH = HEADS_PER_GROUP * HEAD_DIM
DILATED_GROUPS = ((128, 1), (512, 4), (2048, 16))
N_GROUPS = len(DILATED_GROUPS)
BLK = 128
ROPE_THETA = 10000.0
EPS = 1e-6
NEG_INF = -1e30
BIG = 3.0e38
LOG2E = math.log2(math.e)

OFF_A, OFF_B, OFF_GC = 0, 1024, 2048
OFF_Q, OFF_K, OFF_V = 3072, 4608, 6144
OFF_GA, OFF_MC, OFF_MA = 7680, 8192, 9216

LANES = 128
SUB_ROWS = 16
CHUNKS_PER_BLK = BLK // SUB_ROWS
PERM_ROWS = 256
GATHER_STRIDE = 4
HALO = 32
CONV_CHUNK = 8
SIDE_PIECE = 256
SIDE_LAG = 8
VMEM_LIMIT = 56 * 1024 * 1024

TM_QKV = 512
TM_TAIL = 512
PROJ_AHEAD = 2
ATTN_PAIRS = 16
PAIRS_PER_BODY = 4


def _params(n_axes):
    return pltpu.CompilerParams(dimension_semantics=("arbitrary",) * n_axes,
                                vmem_limit_bytes=VMEM_LIMIT)


def _const_spec(shape):
    nd = len(shape)
    return pl.BlockSpec(shape, lambda *_: (0,) * nd, pipeline_mode=pl.Buffered(1))


def _sigmoid(v):
    return jax.nn.sigmoid(v)


def _silu(v):
    return v * jax.nn.sigmoid(v)


def _bdot(a, b):
    return jnp.dot(a, b, preferred_element_type=jnp.float32)


def _zero_from(v):
    bits = pltpu.bitcast(v, jnp.uint32)
    return pltpu.bitcast((bits >> 16) >> 16, jnp.float32)


def _mod_kernel(c_ref, w_ref, b_ref, o_ref):
    sc = _silu(c_ref[...])
    o_ref[...] = jnp.dot(sc, w_ref[...], preferred_element_type=jnp.float32,
                         precision=lax.Precision.HIGHEST) + b_ref[...]


def _mod_call(c, w_ada, b_ada):
    bsz = c.shape[0]
    n_out = w_ada.shape[1]
    bn = 1024
    return pl.pallas_call(
        _mod_kernel,
        grid=(n_out // bn,),
        in_specs=[pl.BlockSpec((bsz, D_MODEL), lambda j: (0, 0)),
                  pl.BlockSpec((D_MODEL, bn), lambda j: (0, j)),
                  pl.BlockSpec((1, bn), lambda j: (0, j))],
        out_specs=pl.BlockSpec((bsz, bn), lambda j: (0, j)),
        out_shape=jax.ShapeDtypeStruct((bsz, n_out), jnp.float32),
        compiler_params=_params(1),
        name="mod",
    )(c, w_ada, b_ada.reshape(1, n_out))


def _normed_input(x, norm_w, scale, shift):
    ms = jnp.mean(x * x, axis=-1, keepdims=True)
    return x * lax.rsqrt(ms + EPS) * norm_w * (1.0 + scale) + shift


def _grouped_shape(bsz, seq, dilation, width):
    return (bsz, seq // (SUB_ROWS * dilation), dilation, SUB_ROWS, width)


def _grouped_tile_spec(tm, dilation, width):
    return pl.BlockSpec((1, tm // (SUB_ROWS * dilation), dilation, SUB_ROWS, width),
                        lambda b, s: (b, s, 0, 0, 0))


def _residue_major(val, dilation, buf_a, buf_b):
    tm = val.shape[0]
    if dilation == 1:
        return val
    buf_a[...] = val
    if dilation == GATHER_STRIDE:
        chunk = SUB_ROWS * dilation
        return jnp.concatenate(
            [buf_a[pl.ds(a * chunk + r, SUB_ROWS, stride=dilation), :]
             for a in range(tm // chunk) for r in range(dilation)], axis=0)
    assert dilation == GATHER_STRIDE * GATHER_STRIDE and PERM_ROWS == SUB_ROWS * dilation
    quarter = PERM_ROWS // GATHER_STRIDE
    for blk in range(tm // PERM_ROWS):
        for r0 in range(GATHER_STRIDE):
            buf_b[blk * PERM_ROWS + r0 * quarter:blk * PERM_ROWS + (r0 + 1) * quarter, :] = (
                buf_a[pl.ds(blk * PERM_ROWS + r0, quarter, stride=GATHER_STRIDE), :])
    return jnp.concatenate(
        [buf_b[pl.ds(blk * PERM_ROWS + (r % GATHER_STRIDE) * quarter + r // GATHER_STRIDE, SUB_ROWS,
                     stride=GATHER_STRIDE), :]
         for blk in range(tm // PERM_ROWS) for r in range(dilation)], axis=0)


def _token_order(val, dilation, buf_a, buf_b):
    tm = val.shape[0]
    if dilation == GATHER_STRIDE:
        chunk = SUB_ROWS * dilation
        for a in range(tm // chunk):
            for r in range(dilation):
                lo = a * chunk + r * SUB_ROWS
                buf_a[pl.ds(a * chunk + r, SUB_ROWS, stride=dilation), :] = val[lo:lo + SUB_ROWS]
        return buf_a[...]
    assert dilation == GATHER_STRIDE * GATHER_STRIDE and PERM_ROWS == SUB_ROWS * dilation
    quarter = PERM_ROWS // GATHER_STRIDE
    for blk in range(tm // PERM_ROWS):
        for r in range(dilation):
            lo = blk * PERM_ROWS + r * SUB_ROWS
            buf_b[pl.ds(blk * PERM_ROWS + (r % GATHER_STRIDE) * quarter + r // GATHER_STRIDE, SUB_ROWS,
                        stride=GATHER_STRIDE), :] = val[lo:lo + SUB_ROWS]
    for blk in range(tm // PERM_ROWS):
        for r0 in range(GATHER_STRIDE):
            lo = blk * PERM_ROWS + r0 * quarter
            buf_a[pl.ds(blk * PERM_ROWS + r0, quarter, stride=GATHER_STRIDE), :] = buf_b[lo:lo + quarter, :]
    return buf_a[...]


def _qkv_kernel(x_ref, shift_ref, scale_ref, nw_ref, wq_ref, wk_ref, wv_ref, qnw_ref, knw_ref, pos_ref,
                freq_ref, *out_and_scratch):
    *out_refs, z_ref, buf_a_ref, buf_b_ref, h_ref = out_and_scratch
    w_refs = (wq_ref, wk_ref, wv_ref)
    x = x_ref[0]
    tm = x.shape[0]
    h_ref[...] = _normed_input(x, nw_ref[...], scale_ref[0], shift_ref[0]).astype(jnp.bfloat16)
    half_rows = tm // 2
    half = HEAD_DIM // 2
    pos = pos_ref[0]
    lane = lax.broadcasted_iota(jnp.int32, (half_rows, HEAD_DIM), 1)
    low = lane < half
    ang = jnp.where(low, pos[:half_rows], pos[half_rows:]) * freq_ref[...]
    cos_p, sin_p = jnp.cos(ang), jnp.sin(ang)
    cos_s, sin_s = pltpu.roll(cos_p, half, axis=1), pltpu.roll(sin_p, half, axis=1)
    cos_t = jnp.concatenate([jnp.where(low, cos_p, cos_s), jnp.where(low, cos_s, cos_p)], axis=0)
    sin_t = jnp.concatenate([jnp.where(low, -sin_p, sin_s), jnp.where(low, -sin_s, sin_p)], axis=0)
    norm_ws = (qnw_ref[...] * (LOG2E * HEAD_DIM ** -0.5), knw_ref[...])
    n_proj = 3 * N_GROUPS

    def project(idx):
        kind, g = divmod(idx, N_GROUPS)
        z_ref[idx] = _bdot(h_ref[...], w_refs[kind][:, g * GROUP_WIDTH:(g + 1) * GROUP_WIDTH])

    for idx in range(PROJ_AHEAD):
        project(idx)
    for idx in range(n_proj):
        kind, g = divmod(idx, N_GROUPS)
        dilation = DILATED_GROUPS[g][1]
        heads = []
        for j in range(HEADS_PER_GROUP):
            zh = z_ref[idx, :, j * HEAD_DIM:(j + 1) * HEAD_DIM]
            if kind < 2:
                ms = jnp.mean(zh * zh, axis=-1, keepdims=True)
                zn = zh * lax.rsqrt(ms + EPS) * norm_ws[kind]
                zh = zn * cos_t + pltpu.roll(zn, half, axis=1) * sin_t
            heads.append(_residue_major(zh, dilation, buf_a_ref.at[j], buf_b_ref.at[j]).astype(jnp.bfloat16))
        o_ref = out_refs[g]
        c0 = kind * GROUP_WIDTH
        o_ref[0, :, :, :, c0:c0 + GROUP_WIDTH] = (
            jnp.concatenate(heads, axis=-1).reshape(o_ref.shape[1:-1] + (GROUP_WIDTH,)))
        if idx + PROJ_AHEAD < n_proj:
            done = pltpu.bitcast(o_ref[0, 0, 0, :, c0:c0 + LANES], jnp.uint32)
            words = pltpu.bitcast(h_ref[0:SUB_ROWS, 0:LANES], jnp.uint32) + ((done >> 16) >> 16)
            h_ref[0:SUB_ROWS, 0:LANES] = pltpu.bitcast(words, jnp.bfloat16)
            project(idx + PROJ_AHEAD)


def _weight_cols_spec(offset, width):
    assert offset % width == 0
    return pl.BlockSpec((D_MODEL, width), lambda *_: (0, offset // width), pipeline_mode=pl.Buffered(1))


def _qkv_call(x, mod3, norm_w, w_in_bf, q_norm_w, k_norm_w, posf, freq2):
    bsz, seq, _ = x.shape
    tm = TM_QKV
    tok = lambda b, s: (b, s, 0)
    dils = [d for _, d in DILATED_GROUPS]
    qkv_width = N_GROUPS * GROUP_WIDTH
    return pl.pallas_call(
        _qkv_kernel,
        grid=(bsz, seq // tm),
        in_specs=[pl.BlockSpec((1, tm, D_MODEL), tok),
                  pl.BlockSpec((1, 1, D_MODEL), lambda b, s: (3 * b, 0, 0)),
                  pl.BlockSpec((1, 1, D_MODEL), lambda b, s: (3 * b + 1, 0, 0)),
                  _const_spec((1, D_MODEL)),
                  _weight_cols_spec(OFF_Q, qkv_width),
                  _weight_cols_spec(OFF_K, qkv_width),
                  _weight_cols_spec(OFF_V, qkv_width),
                  _const_spec((1, HEAD_DIM)),
                  _const_spec((1, HEAD_DIM)),
                  pl.BlockSpec((1, tm, HEAD_DIM), tok),
                  _const_spec((1, HEAD_DIM))],
        out_specs=[_grouped_tile_spec(tm, d, qkv_width) for d in dils],
        out_shape=[jax.ShapeDtypeStruct(_grouped_shape(bsz, seq, d, qkv_width), jnp.bfloat16)
                   for d in dils],
        scratch_shapes=[pltpu.VMEM((3 * N_GROUPS, tm, GROUP_WIDTH), jnp.float32),
                        pltpu.VMEM((HEADS_PER_GROUP, tm, HEAD_DIM), jnp.float32),
                        pltpu.VMEM((HEADS_PER_GROUP, tm, HEAD_DIM), jnp.float32),
                        pltpu.VMEM((tm, D_MODEL), jnp.bfloat16)],
        compiler_params=_params(2),
        name="qkv",
    )(x, mod3, mod3, norm_w.reshape(1, D_MODEL), w_in_bf, w_in_bf, w_in_bf,
      q_norm_w.reshape(1, HEAD_DIM), k_norm_w.reshape(1, HEAD_DIM), posf, freq2)


def _attn_kernel(q_ref, k_ref, v_ref, kp_ref, vp_ref, cap_ref, o_ref, l_ref, *, nb_step, res_step):
    no_prev = (pl.program_id(1) == 0).astype(jnp.int32)
    ones = jnp.ones((BLK, HEAD_DIM), jnp.bfloat16)
    lane = lax.broadcasted_iota(jnp.int32, (BLK, HEAD_DIM), 1)
    nt_dims = (((1,), (1,)), ((), ()))

    def tile(ref, chunk0, r, j):
        t = ref[0, pl.ds(chunk0, CHUNKS_PER_BLK), r, :, j * HEAD_DIM:(j + 1) * HEAD_DIM]
        return t.reshape(BLK, HEAD_DIM)

    def process(pairs, first):
        units = [(i, r, j) for i, r in pairs for j in range(HEADS_PER_GROUP)]
        cap = cap_ref[no_prev] if first else cap_ref[0]
        q = [tile(q_ref, i * CHUNKS_PER_BLK, r, j) for i, r, j in units]
        kc = [tile(k_ref, i * CHUNKS_PER_BLK, r, j) for i, r, j in units]
        vc = [tile(v_ref, i * CHUNKS_PER_BLK, r, j) for i, r, j in units]
        if first:
            kp = [tile(kp_ref, 0, r, j) for _, r, j in units]
            vp = [tile(vp_ref, 0, r, j) for _, r, j in units]
        else:
            kp = [tile(k_ref, (i - 1) * CHUNKS_PER_BLK, r, j) for i, r, j in units]
            vp = [tile(v_ref, (i - 1) * CHUNKS_PER_BLK, r, j) for i, r, j in units]
        s = [lax.dot_general(a, jnp.concatenate([b, c], axis=0), nt_dims, preferred_element_type=jnp.float32)
             for a, b, c in zip(q, kp, kc)]
        s = [jnp.minimum(t, cap) for t in s]
        m = [jnp.max(t, axis=-1, keepdims=True) for t in s]
        p = [jnp.exp2(t - mm).astype(jnp.bfloat16) for t, mm in zip(s, m)]
        oe = [_bdot(t, jnp.concatenate([jnp.concatenate([va, ones], axis=-1),
                                        jnp.concatenate([vb, ones], axis=-1)], axis=0))
              for t, va, vb in zip(p, vp, vc)]
        for n, (i, r) in enumerate(pairs):
            chunks = pl.ds(i * CHUNKS_PER_BLK, CHUNKS_PER_BLK)
            stats = jnp.zeros((BLK, HEAD_DIM), jnp.float32)
            for j in range(HEADS_PER_GROUP):
                u = n * HEADS_PER_GROUP + j
                o_ref[0, chunks, r, :, j * HEAD_DIM:(j + 1) * HEAD_DIM] = (
                    oe[u][:, :HEAD_DIM].astype(o_ref.dtype).reshape(CHUNKS_PER_BLK, SUB_ROWS, HEAD_DIM))
                stats = jnp.where(lane == j, m[u], stats)
                stats = jnp.where(lane == HEADS_PER_GROUP + j, oe[u][:, HEAD_DIM:], stats)
            l_ref[0, chunks, r, :, :] = stats.reshape(CHUNKS_PER_BLK, SUB_ROWS, HEAD_DIM)

    def run(n_pairs, pair_of, first):
        left = n_pairs % PAIRS_PER_BODY
        if left:
            process([pair_of(t) for t in range(left)], first)
        if n_pairs >= PAIRS_PER_BODY:
            def body(t, carry):
                t0 = left + PAIRS_PER_BODY * t
                process([pair_of(t0 + n) for n in range(PAIRS_PER_BODY)], first)
                return carry
            lax.fori_loop(0, n_pairs // PAIRS_PER_BODY, body, 0)

    shift = res_step.bit_length() - 1

    def later_pair(t):
        return 1 + (t >> shift), t & (res_step - 1)

    run(res_step, lambda t: (0, t), True)
    run((nb_step - 1) * res_step, later_pair, False)


def _attn_call(qkv, caps, dilation):
    bsz, n_chunks = qkv.shape[:2]
    nb = n_chunks // CHUNKS_PER_BLK
    nb_step = min(nb, ATTN_PAIRS)
    res_step = min(ATTN_PAIRS // nb_step, dilation)
    assert res_step & (res_step - 1) == 0 and nb % nb_step == 0 and dilation % res_step == 0

    def cur(width, col=0):
        return pl.BlockSpec((1, nb_step * CHUNKS_PER_BLK, res_step, SUB_ROWS, width),
                            lambda b, n, rc: (b, n, rc, 0, col))

    def prev(col):
        return pl.BlockSpec((1, CHUNKS_PER_BLK, res_step, SUB_ROWS, GROUP_WIDTH),
                            lambda b, n, rc: (b, jnp.maximum(n * nb_step - 1, 0), rc, 0, col))

    group_shape = qkv.shape[:-1]
    return pl.pallas_call(
        functools.partial(_attn_kernel, nb_step=nb_step, res_step=res_step),
        grid=(bsz, nb // nb_step, dilation // res_step),
        in_specs=[cur(GROUP_WIDTH, 0), cur(GROUP_WIDTH, 1), cur(GROUP_WIDTH, 2), prev(1), prev(2),
                  _const_spec(caps.shape)],
        out_specs=[cur(GROUP_WIDTH), cur(HEAD_DIM)],
        out_shape=[jax.ShapeDtypeStruct(group_shape + (GROUP_WIDTH,), jnp.bfloat16),
                   jax.ShapeDtypeStruct(group_shape + (HEAD_DIM,), jnp.float32)],
        compiler_params=_params(3),
        name=f"attn_d{dilation}",
    )(qkv, qkv, qkv, qkv, qkv, caps)


def _tail_kernel(x_ref, shift_ref, scale_ref, gate_ref, nw_ref,
                 o0_ref, o1_ref, o2_ref, l0_ref, l1_ref, l2_ref,
                 wa_ref, wb_ref, wgc_ref, wga_ref, wmc_ref, wma_ref,
                 cw_ref, cb_ref, lnw_ref, lnb_ref, wco_ref, wao_ref, wout_ref,
                 out_ref, useq_ref, cseq_ref, side_ref, tok_a_ref, tok_b_ref, tc_ref):
    tm = x_ref.shape[1]
    nsl = D_MODEL // LANES

    @pl.when(pl.program_id(1) == 0)
    def _():
        useq_ref[0:HALO * nsl, :] = jnp.zeros((HALO * nsl, LANES), jnp.float32)

    x = x_ref[0]
    h = _normed_input(x, nw_ref[...], scale_ref[0], shift_ref[0]).astype(jnp.bfloat16)

    for c0 in range(0, D_MODEL, SIDE_PIECE):
        u = _bdot(h, wa_ref[:, c0:c0 + SIDE_PIECE]) * _sigmoid(_bdot(h, wb_ref[:, c0:c0 + SIDE_PIECE]))
        for s in range(SIDE_PIECE // LANES):
            row0 = HALO * nsl + c0 // LANES + s
            useq_ref[pl.ds(row0, tm, stride=nsl), :] = u[:, s * LANES:(s + 1) * LANES]
    def side_piece(w_ref, c0, dst):
        side_ref[:, dst:dst + SIDE_PIECE] = _bdot(h, w_ref[:, c0:c0 + SIDE_PIECE])
        return lambda: side_ref[0:nsl, dst:dst + LANES]

    early_jobs, late_jobs, col = [], [], 0
    for w_ref, early in ((wgc_ref, True), (wga_ref, True), (wmc_ref, False), (wma_ref, False)):
        for c0 in range(0, w_ref.shape[1], SIDE_PIECE):
            (early_jobs if early else late_jobs).append(
                functools.partial(side_piece, w_ref, c0, col + c0))
        col += w_ref.shape[1]
    n_chunks = tm // CONV_CHUNK
    parked = {}
    for c in range(n_chunks):
        base = c * CONV_CHUNK * nsl
        bias = cb_ref[...]
        for read_back in parked.get(c - SIDE_LAG, ()):
            bias = bias + _zero_from(read_back())
        acc = jnp.broadcast_to(bias[None], (CONV_CHUNK, nsl, LANES))
        for t in range(CONV_KERNEL):
            start = base + (HALO - (CONV_KERNEL - 1) + t) * nsl
            taps = useq_ref[pl.ds(start, CONV_CHUNK * nsl), :].reshape(CONV_CHUNK, nsl, LANES)
            acc = acc + cw_ref[t][None] * taps
        cseq_ref[pl.ds(base, CONV_CHUNK * nsl), :] = acc.reshape(CONV_CHUNK * nsl, LANES)
        lo, hi = len(early_jobs) * c // n_chunks, len(early_jobs) * (c + 1) // n_chunks
        parked[c] = [job() for job in early_jobs[lo:hi]]
    late_reads = [job() for job in late_jobs]
    z_gc = side_ref[:, 0:D_MODEL]
    z_ga = side_ref[:, D_MODEL:D_MODEL + GROUP_WIDTH]
    z_mc = side_ref[:, D_MODEL + GROUP_WIDTH:2 * D_MODEL + GROUP_WIDTH]
    z_ma = side_ref[:, 2 * D_MODEL + GROUP_WIDTH:3 * D_MODEL + GROUP_WIDTH]
    useq_ref[0:HALO * nsl, :] = useq_ref[tm * nsl:(tm + HALO) * nsl, :]
    acc = jnp.concatenate([cseq_ref[pl.ds(s, tm, stride=nsl), :] for s in range(nsl)], axis=-1)
    mu = jnp.mean(acc, axis=-1, keepdims=True)
    cen = acc - mu
    var = jnp.mean(cen * cen, axis=-1, keepdims=True)
    ln = cen * lax.rsqrt(var + EPS) * lnw_ref[...] + lnb_ref[...]
    tc_ref[...] = (_silu(ln) * _silu(z_gc)).astype(jnp.bfloat16)
    words = pltpu.bitcast(tc_ref[0:SUB_ROWS, 0:LANES], jnp.uint32)
    for read_back in late_reads:
        words = words + ((pltpu.bitcast(read_back(), jnp.uint32) >> 16) >> 16)
    tc_ref[0:SUB_ROWS, 0:LANES] = pltpu.bitcast(words, jnp.bfloat16)
    y_conv = _bdot(tc_ref[...], wco_ref[...])

    o_g, stats, slot = [], [], 0
    for g, (o_ref, l_ref) in enumerate(((o0_ref, l0_ref), (o1_ref, l1_ref), (o2_ref, l2_ref))):
        dilation = DILATED_GROUPS[g][1]
        o_rows = o_ref[0].reshape(tm, GROUP_WIDTH).astype(jnp.float32)
        slabs = [o_rows[:, j * HEAD_DIM:(j + 1) * HEAD_DIM] for j in range(HEADS_PER_GROUP)]
        slabs.append(l_ref[0].reshape(tm, HEAD_DIM))
        if dilation > 1:
            for n, slab in enumerate(slabs):
                slabs[n] = _token_order(slab, dilation, tok_a_ref.at[slot], tok_b_ref.at[slot])
                slot += 1
        o_g.append(jnp.concatenate(slabs[:HEADS_PER_GROUP], axis=-1))
        stats.append(slabs[HEADS_PER_GROUP])
    top = jnp.maximum(jnp.maximum(stats[0], stats[1]), stats[2])
    e_g = [jnp.exp2(st - top) for st in stats]
    dens = [pltpu.roll(st, LANES - HEADS_PER_GROUP, axis=1) for st in stats]
    total = e_g[0] * dens[0] + e_g[1] * dens[1] + e_g[2] * dens[2]
    head_lane = lax.broadcasted_iota(jnp.int32, total.shape, 1) < HEADS_PER_GROUP
    inv = 1.0 / jnp.where(head_lane, total, 1.0)
    w_g = [e * inv for e in e_g]
    heads = []
    for j in range(HEADS_PER_GROUP):
        cols = slice(j * HEAD_DIM, (j + 1) * HEAD_DIM)
        heads.append(sum(w[:, j:j + 1] * o[:, cols] for w, o in zip(w_g, o_g)))
    o = jnp.concatenate(heads, axis=-1)
    ta = (o * _silu(z_ga)).astype(jnp.bfloat16)
    y_attn = _bdot(ta, wao_ref[...])

    y = _sigmoid(z_mc) * y_conv + _sigmoid(z_ma) * y_attn
    out = _bdot(y.astype(jnp.bfloat16), wout_ref[...])
    out_ref[0] = x + gate_ref[0] * out


def _tail_call(x, mod3, norm_w, o_groups, l_groups, w_in_bf, conv_w, conv_b, ln_w, ln_b,
               w_co, w_ao, w_out):
    side_cols = ((OFF_A, D_MODEL), (OFF_B, D_MODEL), (OFF_GC, D_MODEL),
                 (OFF_GA, GROUP_WIDTH), (OFF_MC, D_MODEL), (OFF_MA, D_MODEL))
    bsz, seq, _ = x.shape
    tm = TM_TAIL
    nsl = D_MODEL // LANES
    tok = lambda b, s: (b, s, 0)
    row = lambda v: v.reshape(1, D_MODEL)
    modspec = lambda k: pl.BlockSpec((1, 1, D_MODEL), lambda b, s: (3 * b + k, 0, 0))
    dils = [d for _, d in DILATED_GROUPS]
    n_relayout = (HEADS_PER_GROUP + 1) * sum(d > 1 for d in dils)
    in_specs = ([pl.BlockSpec((1, tm, D_MODEL), tok), modspec(0), modspec(1), modspec(2),
                 _const_spec((1, D_MODEL))]
                + [_grouped_tile_spec(tm, d, GROUP_WIDTH) for d in dils]
                + [_grouped_tile_spec(tm, d, HEAD_DIM) for d in dils]
                + [_weight_cols_spec(off, width) for off, width in side_cols]
                + [_const_spec((CONV_KERNEL, nsl, LANES)), _const_spec((nsl, LANES))]
                + [_const_spec((1, D_MODEL))] * 2
                + [_const_spec(w_co.shape), _const_spec(w_ao.shape), _const_spec(w_out.shape)])
    return pl.pallas_call(
        _tail_kernel,
        grid=(bsz, seq // tm),
        in_specs=in_specs,
        out_specs=pl.BlockSpec((1, tm, D_MODEL), tok),
        out_shape=jax.ShapeDtypeStruct(x.shape, x.dtype),
        scratch_shapes=[pltpu.VMEM(((tm + HALO) * nsl, LANES), jnp.float32),
                        pltpu.VMEM((tm * nsl, LANES), jnp.float32),
                        pltpu.VMEM((tm, 3 * D_MODEL + GROUP_WIDTH), jnp.float32),
                        pltpu.VMEM((n_relayout, tm, LANES), jnp.float32),
                        pltpu.VMEM((n_relayout, tm, LANES), jnp.float32),
                        pltpu.VMEM((tm, D_MODEL), jnp.bfloat16)],
        compiler_params=_params(2),
        name="tail",
    )(x, mod3, mod3, mod3, row(norm_w), *o_groups, *l_groups, *([w_in_bf] * len(side_cols)),
      conv_w.reshape(CONV_KERNEL, nsl, LANES), conv_b.reshape(nsl, LANES), row(ln_w), row(ln_b),
      w_co, w_ao, w_out)


def _attention_caps():
    qi = np.arange(BLK)[:, None]
    kj = np.arange(BLK)[None, :]
    prev = np.where(kj >= qi, BIG, NEG_INF)
    cur = np.where(kj <= qi, BIG, NEG_INF)
    no_prev = np.full((BLK, BLK), NEG_INF)
    caps = np.stack([np.concatenate([prev, cur], axis=1),
                     np.concatenate([no_prev, cur], axis=1)]).astype(np.float32)
    return jnp.asarray(caps)


def kernel(x, c, positions, norm_w, w_ada, b_ada, w_in, conv_w, conv_b, conv_ln_w, conv_ln_b,
           w_conv_out, q_norm_w, k_norm_w, w_attn_out, w_out):
    bsz, seq, _ = x.shape
    bf = jnp.bfloat16
    for window, dilation in DILATED_GROUPS:
        assert window // dilation == BLK and seq % (dilation * BLK) == 0
        assert PERM_ROWS % (SUB_ROWS * dilation) == 0
    assert seq % TM_QKV == 0 and seq % TM_TAIL == 0
    assert TM_QKV % PERM_ROWS == 0 and TM_TAIL % PERM_ROWS == 0

    mod3 = _mod_call(c, w_ada, b_ada).reshape(bsz * 3, 1, D_MODEL)

    w_in_bf = w_in.astype(bf)
    inv_freq = ROPE_THETA ** (-jnp.arange(0, HEAD_DIM, 2, dtype=jnp.float32) / HEAD_DIM)
    freq2 = jnp.concatenate([inv_freq, inv_freq]).reshape(1, HEAD_DIM)
    posf = jnp.broadcast_to(positions.astype(jnp.float32)[..., None], (bsz, seq, HEAD_DIM))
    qkv = _qkv_call(x, mod3, norm_w, w_in_bf, q_norm_w, k_norm_w, posf, freq2)

    caps = _attention_caps()
    o_groups, l_groups = [], []
    for g, (_, dilation) in enumerate(DILATED_GROUPS):
        o_g, l_g = _attn_call(qkv[g], caps, dilation)
        o_groups.append(o_g)
        l_groups.append(l_g)

    return _tail_call(x, mod3, norm_w, o_groups, l_groups, w_in_bf,
                      conv_w, conv_b, conv_ln_w, conv_ln_b,
                      w_conv_out.astype(bf), w_attn_out.astype(bf), w_out.astype(bf))
```

```python
import functools
import math

import jax
import jax.numpy as jnp
import numpy as np
from jax import lax
from jax.experimental import pallas as pl
from jax.experimental.pallas import tpu as pltpu

D_MODEL = 1024
CONV_KERNEL = 31
HEAD_DIM = 128
HEADS_PER_GROUP = 4
GROUP_WIDTH = HEADS_PER_GROUP * HEAD_DIM
DILATED_GROUPS = ((128, 1), (512, 4), (2048, 16))
N_GROUPS = len(DILATED_GROUPS)
BLK = 128
ROPE_THETA = 10000.0
EPS = 1e-6
NEG_INF = -1e30
BIG = 3.0e38
LOG2E = math.log2(math.e)

OFF_A, OFF_B, OFF_GC = 0, 1024, 2048
OFF_Q, OFF_K, OFF_V = 3072, 4608, 6144
OFF_GA, OFF_MC, OFF_MA = 7680, 8192, 9216

LANES = 128
SUB_ROWS = 16
CHUNKS_PER_BLK = BLK // SUB_ROWS
PERM_ROWS = 256
GATHER_STRIDE = 4
HALO = 32
CONV_CHUNK = 8
SIDE_PIECE = 256
SIDE_LAG = 16
VMEM_LIMIT = 56 * 1024 * 1024

TM_QKV = 512
TM_TAIL = 512
PROJ_AHEAD = 3
ATTN_PAIRS = 16
PAIRS_PER_BODY = 4


def _params(n_axes):
    return pltpu.CompilerParams(dimension_semantics=("arbitrary",) * n_axes,
                                vmem_limit_bytes=VMEM_LIMIT)


def _const_spec(shape):
    nd = len(shape)
    return pl.BlockSpec(shape, lambda *_: (0,) * nd, pipeline_mode=pl.Buffered(1))


def _sigmoid(v):
    return jax.nn.sigmoid(v)


def _silu(v):
    return v * jax.nn.sigmoid(v)


def _bdot(a, b):
    return jnp.dot(a, b, preferred_element_type=jnp.float32)


def _zero_from(v):
    bits = pltpu.bitcast(v, jnp.uint32)
    return pltpu.bitcast((bits >> 16) >> 16, jnp.float32)


def _mod_kernel(c_ref, w_ref, b_ref, o_ref):
    sc = _silu(c_ref[...])
    o_ref[...] = jnp.dot(sc, w_ref[...], preferred_element_type=jnp.float32,
                         precision=lax.Precision.HIGHEST) + b_ref[...]


def _mod_call(c, w_ada, b_ada):
    bsz = c.shape[0]
    n_out = w_ada.shape[1]
    bn = 1024
    return pl.pallas_call(
        _mod_kernel,
        grid=(n_out // bn,),
        in_specs=[pl.BlockSpec((bsz, D_MODEL), lambda j: (0, 0)),
                  pl.BlockSpec((D_MODEL, bn), lambda j: (0, j)),
                  pl.BlockSpec((1, bn), lambda j: (0, j))],
        out_specs=pl.BlockSpec((bsz, bn), lambda j: (0, j)),
        out_shape=jax.ShapeDtypeStruct((bsz, n_out), jnp.float32),
        compiler_params=_params(1),
        name="mod",
    )(c, w_ada, b_ada.reshape(1, n_out))


def _normed_input(x, norm_w, scale, shift):
    ms = jnp.mean(x * x, axis=-1, keepdims=True)
    return x * lax.rsqrt(ms + EPS) * norm_w * (1.0 + scale) + shift


def _grouped_shape(bsz, seq, dilation, width):
    return (bsz, seq // (SUB_ROWS * dilation), dilation, SUB_ROWS, width)


def _grouped_tile_spec(tm, dilation, width):
    return pl.BlockSpec((1, tm // (SUB_ROWS * dilation), dilation, SUB_ROWS, width),
                        lambda b, s: (b, s, 0, 0, 0))


def _residue_major(val, dilation, buf_a, buf_b):
    tm = val.shape[0]
    if dilation == 1:
        return val
    buf_a[...] = val
    if dilation == GATHER_STRIDE:
        chunk = SUB_ROWS * dilation
        return jnp.concatenate(
            [buf_a[pl.ds(a * chunk + r, SUB_ROWS, stride=dilation), :]
             for a in range(tm // chunk) for r in range(dilation)], axis=0)
    assert dilation == GATHER_STRIDE * GATHER_STRIDE and PERM_ROWS == SUB_ROWS * dilation
    quarter = PERM_ROWS // GATHER_STRIDE
    for blk in range(tm // PERM_ROWS):
        for r0 in range(GATHER_STRIDE):
            buf_b[blk * PERM_ROWS + r0 * quarter:blk * PERM_ROWS + (r0 + 1) * quarter, :] = (
                buf_a[pl.ds(blk * PERM_ROWS + r0, quarter, stride=GATHER_STRIDE), :])
    return jnp.concatenate(
        [buf_b[pl.ds(blk * PERM_ROWS + (r % GATHER_STRIDE) * quarter + r // GATHER_STRIDE, SUB_ROWS,
                     stride=GATHER_STRIDE), :]
         for blk in range(tm // PERM_ROWS) for r in range(dilation)], axis=0)


def _token_order(val, dilation, buf_a, buf_b):
    tm = val.shape[0]
    if dilation == GATHER_STRIDE:
        chunk = SUB_ROWS * dilation
        for a in range(tm // chunk):
            for r in range(dilation):
                lo = a * chunk + r * SUB_ROWS
                buf_a[pl.ds(a * chunk + r, SUB_ROWS, stride=dilation), :] = val[lo:lo + SUB_ROWS]
        return buf_a[...]
    assert dilation == GATHER_STRIDE * GATHER_STRIDE and PERM_ROWS == SUB_ROWS * dilation
    quarter = PERM_ROWS // GATHER_STRIDE
    for blk in range(tm // PERM_ROWS):
        for r in range(dilation):
            lo = blk * PERM_ROWS + r * SUB_ROWS
            buf_b[pl.ds(blk * PERM_ROWS + (r % GATHER_STRIDE) * quarter + r // GATHER_STRIDE, SUB_ROWS,
                        stride=GATHER_STRIDE), :] = val[lo:lo + SUB_ROWS]
    for blk in range(tm // PERM_ROWS):
        for r0 in range(GATHER_STRIDE):
            lo = blk * PERM_ROWS + r0 * quarter
            buf_a[pl.ds(blk * PERM_ROWS + r0, quarter, stride=GATHER_STRIDE), :] = buf_b[lo:lo + quarter, :]
    return buf_a[...]


def _qkv_kernel(x_ref, shift_ref, scale_ref, nw_ref, wq_ref, wk_ref, wv_ref, qnw_ref, knw_ref, pos_ref,
                freq_ref, *out_and_scratch):
    *out_refs, z_ref, buf_a_ref, buf_b_ref, h_ref = out_and_scratch
    w_refs = (wq_ref, wk_ref, wv_ref)
    x = x_ref[0]
    tm = x.shape[0]
    h_ref[...] = _normed_input(x, nw_ref[...], scale_ref[0], shift_ref[0]).astype(jnp.bfloat16)
    half_rows = tm // 2
    half = HEAD_DIM // 2
    pos = pos_ref[0]
    lane = lax.broadcasted_iota(jnp.int32, (half_rows, HEAD_DIM), 1)
    low = lane < half
    ang = jnp.where(low, pos[:half_rows], pos[half_rows:]) * freq_ref[...]
    cos_p, sin_p = jnp.cos(ang), jnp.sin(ang)
    cos_s, sin_s = pltpu.roll(cos_p, half, axis=1), pltpu.roll(sin_p, half, axis=1)
    cos_t = jnp.concatenate([jnp.where(low, cos_p, cos_s), jnp.where(low, cos_s, cos_p)], axis=0)
    sin_t = jnp.concatenate([jnp.where(low, -sin_p, sin_s), jnp.where(low, -sin_s, sin_p)], axis=0)
    norm_ws = (qnw_ref[...] * (LOG2E * HEAD_DIM ** -0.5), knw_ref[...])
    n_proj = 3 * N_GROUPS

    def project(idx):
        kind, g = divmod(idx, N_GROUPS)
        z_ref[idx] = _bdot(h_ref[...], w_refs[kind][:, g * GROUP_WIDTH:(g + 1) * GROUP_WIDTH])

    for idx in range(PROJ_AHEAD):
        project(idx)
    for idx in range(n_proj):
        kind, g = divmod(idx, N_GROUPS)
        dilation = DILATED_GROUPS[g][1]
        heads = []
        for j in range(HEADS_PER_GROUP):
            zh = z_ref[idx, :, j * HEAD_DIM:(j + 1) * HEAD_DIM]
            if kind < 2:
                ms = jnp.mean(zh * zh, axis=-1, keepdims=True)
                zn = zh * lax.rsqrt(ms + EPS) * norm_ws[kind]
                zh = zn * cos_t + pltpu.roll(zn, half, axis=1) * sin_t
            heads.append(_residue_major(zh, dilation, buf_a_ref.at[j], buf_b_ref.at[j]).astype(jnp.bfloat16))
        o_ref = out_refs[g]
        c0 = kind * GROUP_WIDTH
        o_ref[0, :, :, :, c0:c0 + GROUP_WIDTH] = (
            jnp.concatenate(heads, axis=-1).reshape(o_ref.shape[1:-1] + (GROUP_WIDTH,)))
        if idx + PROJ_AHEAD < n_proj:
            done = pltpu.bitcast(o_ref[0, 0, 0, :, c0:c0 + LANES], jnp.uint32)
            words = pltpu.bitcast(h_ref[0:SUB_ROWS, 0:LANES], jnp.uint32) + ((done >> 16) >> 16)
            h_ref[0:SUB_ROWS, 0:LANES] = pltpu.bitcast(words, jnp.bfloat16)
            project(idx + PROJ_AHEAD)


def _weight_cols_spec(offset, width):
    assert offset % width == 0
    return pl.BlockSpec((D_MODEL, width), lambda *_: (0, offset // width), pipeline_mode=pl.Buffered(1))


def _qkv_call(x, mod3, norm_w, w_in_bf, q_norm_w, k_norm_w, posf, freq2):
    bsz, seq, _ = x.shape
    tm = TM_QKV
    tok = lambda b, s: (b, s, 0)
    dils = [d for _, d in DILATED_GROUPS]
    qkv_width = N_GROUPS * GROUP_WIDTH
    return pl.pallas_call(
        _qkv_kernel,
        grid=(bsz, seq // tm),
        in_specs=[pl.BlockSpec((1, tm, D_MODEL), tok),
                  pl.BlockSpec((1, 1, D_MODEL), lambda b, s: (3 * b, 0, 0)),
                  pl.BlockSpec((1, 1, D_MODEL), lambda b, s: (3 * b + 1, 0, 0)),
                  _const_spec((1, D_MODEL)),
                  _weight_cols_spec(OFF_Q, qkv_width),
                  _weight_cols_spec(OFF_K, qkv_width),
                  _weight_cols_spec(OFF_V, qkv_width),
                  _const_spec((1, HEAD_DIM)),
                  _const_spec((1, HEAD_DIM)),
                  pl.BlockSpec((1, tm, HEAD_DIM), tok),
                  _const_spec((1, HEAD_DIM))],
        out_specs=[_grouped_tile_spec(tm, d, qkv_width) for d in dils],
        out_shape=[jax.ShapeDtypeStruct(_grouped_shape(bsz, seq, d, qkv_width), jnp.bfloat16)
                   for d in dils],
        scratch_shapes=[pltpu.VMEM((3 * N_GROUPS, tm, GROUP_WIDTH), jnp.float32),
                        pltpu.VMEM((HEADS_PER_GROUP, tm, HEAD_DIM), jnp.float32),
                        pltpu.VMEM((HEADS_PER_GROUP, tm, HEAD_DIM), jnp.float32),
                        pltpu.VMEM((tm, D_MODEL), jnp.bfloat16)],
        compiler_params=_params(2),
        name="qkv",
    )(x, mod3, mod3, norm_w.reshape(1, D_MODEL), w_in_bf, w_in_bf, w_in_bf,
      q_norm_w.reshape(1, HEAD_DIM), k_norm_w.reshape(1, HEAD_DIM), posf, freq2)


def _attn_kernel(q_ref, k_ref, v_ref, kp_ref, vp_ref, cap_ref, o_ref, l_ref, *, nb_step, res_step):
    no_prev = (pl.program_id(1) == 0).astype(jnp.int32)
    ones = jnp.ones((BLK, HEAD_DIM), jnp.bfloat16)
    lane = lax.broadcasted_iota(jnp.int32, (BLK, HEAD_DIM), 1)
    nt_dims = (((1,), (1,)), ((), ()))

    def tile(ref, chunk0, r, j):
        t = ref[0, pl.ds(chunk0, CHUNKS_PER_BLK), r, :, j * HEAD_DIM:(j + 1) * HEAD_DIM]
        return t.reshape(BLK, HEAD_DIM)

    def process(pairs, first):
        units = [(i, r, j) for i, r in pairs for j in range(HEADS_PER_GROUP)]
        cap = cap_ref[no_prev] if first else cap_ref[0]
        q = [tile(q_ref, i * CHUNKS_PER_BLK, r, j) for i, r, j in units]
        kc = [tile(k_ref, i * CHUNKS_PER_BLK, r, j) for i, r, j in units]
        vc = [tile(v_ref, i * CHUNKS_PER_BLK, r, j) for i, r, j in units]
        if first:
            kp = [tile(kp_ref, 0, r, j) for _, r, j in units]
            vp = [tile(vp_ref, 0, r, j) for _, r, j in units]
        else:
            kp = [tile(k_ref, (i - 1) * CHUNKS_PER_BLK, r, j) for i, r, j in units]
            vp = [tile(v_ref, (i - 1) * CHUNKS_PER_BLK, r, j) for i, r, j in units]
        s = [lax.dot_general(a, jnp.concatenate([b, c], axis=0), nt_dims, preferred_element_type=jnp.float32)
             for a, b, c in zip(q, kp, kc)]
        s = [jnp.minimum(t, cap) for t in s]
        m = [jnp.max(t, axis=-1, keepdims=True) for t in s]
        p = [jnp.exp2(t - mm).astype(jnp.bfloat16) for t, mm in zip(s, m)]
        oe = [_bdot(t, jnp.concatenate([jnp.concatenate([va, ones], axis=-1),
                                        jnp.concatenate([vb, ones], axis=-1)], axis=0))
              for t, va, vb in zip(p, vp, vc)]
        for n, (i, r) in enumerate(pairs):
            chunks = pl.ds(i * CHUNKS_PER_BLK, CHUNKS_PER_BLK)
            stats = jnp.zeros((BLK, HEAD_DIM), jnp.float32)
            for j in range(HEADS_PER_GROUP):
                u = n * HEADS_PER_GROUP + j
                o_ref[0, chunks, r, :, j * HEAD_DIM:(j + 1) * HEAD_DIM] = (
                    oe[u][:, :HEAD_DIM].astype(o_ref.dtype).reshape(CHUNKS_PER_BLK, SUB_ROWS, HEAD_DIM))
                stats = jnp.where(lane == j, m[u], stats)
                stats = jnp.where(lane == HEADS_PER_GROUP + j, oe[u][:, HEAD_DIM:], stats)
            l_ref[0, chunks, r, :, :] = stats.reshape(CHUNKS_PER_BLK, SUB_ROWS, HEAD_DIM)

    def run(n_pairs, pair_of, first):
        left = n_pairs % PAIRS_PER_BODY
        if left:
            process([pair_of(t) for t in range(left)], first)
        if n_pairs >= PAIRS_PER_BODY:
            def body(t, carry):
                t0 = left + PAIRS_PER_BODY * t
                process([pair_of(t0 + n) for n in range(PAIRS_PER_BODY)], first)
                return carry
            lax.fori_loop(0, n_pairs // PAIRS_PER_BODY, body, 0)

    shift = res_step.bit_length() - 1

    def later_pair(t):
        return 1 + (t >> shift), t & (res_step - 1)

    run(res_step, lambda t: (0, t), True)
    run((nb_step - 1) * res_step, later_pair, False)


def _attn_call(qkv, caps, dilation):
    bsz, n_chunks = qkv.shape[:2]
    nb = n_chunks // CHUNKS_PER_BLK
    nb_step = min(nb, ATTN_PAIRS)
    res_step = min(ATTN_PAIRS // nb_step, dilation)
    assert res_step & (res_step - 1) == 0 and nb % nb_step == 0 and dilation % res_step == 0

    def cur(width, col=0):
        return pl.BlockSpec((1, nb_step * CHUNKS_PER_BLK, res_step, SUB_ROWS, width),
                            lambda b, n, rc: (b, n, rc, 0, col))

    def prev(col):
        return pl.BlockSpec((1, CHUNKS_PER_BLK, res_step, SUB_ROWS, GROUP_WIDTH),
                            lambda b, n, rc: (b, jnp.maximum(n * nb_step - 1, 0), rc, 0, col))

    group_shape = qkv.shape[:-1]
    return pl.pallas_call(
        functools.partial(_attn_kernel, nb_step=nb_step, res_step=res_step),
        grid=(bsz, nb // nb_step, dilation // res_step),
        in_specs=[cur(GROUP_WIDTH, 0), cur(GROUP_WIDTH, 1), cur(GROUP_WIDTH, 2), prev(1), prev(2),
                  _const_spec(caps.shape)],
        out_specs=[cur(GROUP_WIDTH), cur(HEAD_DIM)],
        out_shape=[jax.ShapeDtypeStruct(group_shape + (GROUP_WIDTH,), jnp.bfloat16),
                   jax.ShapeDtypeStruct(group_shape + (HEAD_DIM,), jnp.float32)],
        compiler_params=_params(3),
        name=f"attn_d{dilation}",
    )(qkv, qkv, qkv, qkv, qkv, caps)


def _tail_kernel(x_ref, shift_ref, scale_ref, gate_ref, nw_ref,
                 o0_ref, o1_ref, o2_ref, l0_ref, l1_ref, l2_ref,
                 wa_ref, wb_ref, wgc_ref, wga_ref, wmc_ref, wma_ref,
                 cw_ref, cb_ref, lnw_ref, lnb_ref, wco_ref, wao_ref, wout_ref,
                 out_ref, useq_ref, cseq_ref, side_ref, tok_a_ref, tok_b_ref, tc_ref):
    tm = x_ref.shape[1]
    nsl = D_MODEL // LANES

    @pl.when(pl.program_id(1) == 0)
    def _():
        useq_ref[0:HALO * nsl, :] = jnp.zeros((HALO * nsl, LANES), jnp.float32)

    x = x_ref[0]
    h = _normed_input(x, nw_ref[...], scale_ref[0], shift_ref[0]).astype(jnp.bfloat16)

    for c0 in range(0, D_MODEL, SIDE_PIECE):
        u = _bdot(h, wa_ref[:, c0:c0 + SIDE_PIECE]) * _sigmoid(_bdot(h, wb_ref[:, c0:c0 + SIDE_PIECE]))
        for s in range(SIDE_PIECE // LANES):
            row0 = HALO * nsl + c0 // LANES + s
            useq_ref[pl.ds(row0, tm, stride=nsl), :] = u[:, s * LANES:(s + 1) * LANES]
    def side_piece(w_ref, c0, dst):
        side_ref[:, dst:dst + SIDE_PIECE] = _bdot(h, w_ref[:, c0:c0 + SIDE_PIECE])
        return lambda: side_ref[0:nsl, dst:dst + LANES]

    early_jobs, late_jobs, col = [], [], 0
    for w_ref, early in ((wgc_ref, True), (wga_ref, True), (wmc_ref, False), (wma_ref, False)):
        for c0 in range(0, w_ref.shape[1], SIDE_PIECE):
            (early_jobs if early else late_jobs).append(
                functools.partial(side_piece, w_ref, c0, col + c0))
        col += w_ref.shape[1]
    n_chunks = tm // CONV_CHUNK
    parked = {}
    for c in range(n_chunks):
        base = c * CONV_CHUNK * nsl
        bias = cb_ref[...]
        for read_back in parked.get(c - SIDE_LAG, ()):
            bias = bias + _zero_from(read_back())
        acc = jnp.broadcast_to(bias[None], (CONV_CHUNK, nsl, LANES))
        for t in range(CONV_KERNEL):
            start = base + (HALO - (CONV_KERNEL - 1) + t) * nsl
            taps = useq_ref[pl.ds(start, CONV_CHUNK * nsl), :].reshape(CONV_CHUNK, nsl, LANES)
            acc = acc + cw_ref[t][None] * taps
        cseq_ref[pl.ds(base, CONV_CHUNK * nsl), :] = acc.reshape(CONV_CHUNK * nsl, LANES)
        lo, hi = len(early_jobs) * c // n_chunks, len(early_jobs) * (c + 1) // n_chunks
        parked[c] = [job() for job in early_jobs[lo:hi]]
    late_reads = [job() for job in late_jobs]
    z_gc = side_ref[:, 0:D_MODEL]
    z_ga = side_ref[:, D_MODEL:D_MODEL + GROUP_WIDTH]
    z_mc = side_ref[:, D_MODEL + GROUP_WIDTH:2 * D_MODEL + GROUP_WIDTH]
    z_ma = side_ref[:, 2 * D_MODEL + GROUP_WIDTH:3 * D_MODEL + GROUP_WIDTH]
    useq_ref[0:HALO * nsl, :] = useq_ref[tm * nsl:(tm + HALO) * nsl, :]
    acc = jnp.concatenate([cseq_ref[pl.ds(s, tm, stride=nsl), :] for s in range(nsl)], axis=-1)
    mu = jnp.mean(acc, axis=-1, keepdims=True)
    cen = acc - mu
    var = jnp.mean(cen * cen, axis=-1, keepdims=True)
    ln = cen * lax.rsqrt(var + EPS) * lnw_ref[...] + lnb_ref[...]
    tc_ref[...] = (_silu(ln) * _silu(z_gc)).astype(jnp.bfloat16)
    words = pltpu.bitcast(tc_ref[0:SUB_ROWS, 0:LANES], jnp.uint32)
    for read_back in late_reads:
        words = words + ((pltpu.bitcast(read_back(), jnp.uint32) >> 16) >> 16)
    tc_ref[0:SUB_ROWS, 0:LANES] = pltpu.bitcast(words, jnp.bfloat16)
    y_conv = _bdot(tc_ref[...], wco_ref[...])

    o_g, stats, slot = [], [], 0
    for g, (o_ref, l_ref) in enumerate(((o0_ref, l0_ref), (o1_ref, l1_ref), (o2_ref, l2_ref))):
        dilation = DILATED_GROUPS[g][1]
        o_rows = o_ref[0].reshape(tm, GROUP_WIDTH).astype(jnp.float32)
        slabs = [o_rows[:, j * HEAD_DIM:(j + 1) * HEAD_DIM] for j in range(HEADS_PER_GROUP)]
        slabs.append(l_ref[0].reshape(tm, HEAD_DIM))
        if dilation > 1:
            for n, slab in enumerate(slabs):
                slabs[n] = _token_order(slab, dilation, tok_a_ref.at[slot], tok_b_ref.at[slot])
                slot += 1
        o_g.append(jnp.concatenate(slabs[:HEADS_PER_GROUP], axis=-1))
        stats.append(slabs[HEADS_PER_GROUP])
    top = jnp.maximum(jnp.maximum(stats[0], stats[1]), stats[2])
    e_g = [jnp.exp2(st - top) for st in stats]
    dens = [pltpu.roll(st, LANES - HEADS_PER_GROUP, axis=1) for st in stats]
    total = e_g[0] * dens[0] + e_g[1] * dens[1] + e_g[2] * dens[2]
    head_lane = lax.broadcasted_iota(jnp.int32, total.shape, 1) < HEADS_PER_GROUP
    inv = 1.0 / jnp.where(head_lane, total, 1.0)
    w_g = [e * inv for e in e_g]
    heads = []
    for j in range(HEADS_PER_GROUP):
        cols = slice(j * HEAD_DIM, (j + 1) * HEAD_DIM)
        heads.append(sum(w[:, j:j + 1] * o[:, cols] for w, o in zip(w_g, o_g)))
    o = jnp.concatenate(heads, axis=-1)
    ta = (o * _silu(z_ga)).astype(jnp.bfloat16)
    y_attn = _bdot(ta, wao_ref[...])

    y = _sigmoid(z_mc) * y_conv + _sigmoid(z_ma) * y_attn
    out = _bdot(y.astype(jnp.bfloat16), wout_ref[...])
    out_ref[0] = x + gate_ref[0] * out


def _tail_call(x, mod3, norm_w, o_groups, l_groups, w_in_bf, conv_w, conv_b, ln_w, ln_b,
               w_co, w_ao, w_out):
    side_cols = ((OFF_A, D_MODEL), (OFF_B, D_MODEL), (OFF_GC, D_MODEL),
                 (OFF_GA, GROUP_WIDTH), (OFF_MC, D_MODEL), (OFF_MA, D_MODEL))
    bsz, seq, _ = x.shape
    tm = TM_TAIL
    nsl = D_MODEL // LANES
    tok = lambda b, s: (b, s, 0)
    row = lambda v: v.reshape(1, D_MODEL)
    modspec = lambda k: pl.BlockSpec((1, 1, D_MODEL), lambda b, s: (3 * b + k, 0, 0))
    dils = [d for _, d in DILATED_GROUPS]
    n_relayout = (HEADS_PER_GROUP + 1) * sum(d > 1 for d in dils)
    in_specs = ([pl.BlockSpec((1, tm, D_MODEL), tok), modspec(0), modspec(1), modspec(2),
                 _const_spec((1, D_MODEL))]
                + [_grouped_tile_spec(tm, d, GROUP_WIDTH) for d in dils]
                + [_grouped_tile_spec(tm, d, HEAD_DIM) for d in dils]
                + [_weight_cols_spec(off, width) for off, width in side_cols]
                + [_const_spec((CONV_KERNEL, nsl, LANES)), _const_spec((nsl, LANES))]
                + [_const_spec((1, D_MODEL))] * 2
                + [_const_spec(w_co.shape), _const_spec(w_ao.shape), _const_spec(w_out.shape)])
    return pl.pallas_call(
        _tail_kernel,
        grid=(bsz, seq // tm),
        in_specs=in_specs,
        out_specs=pl.BlockSpec((1, tm, D_MODEL), tok),
        out_shape=jax.ShapeDtypeStruct(x.shape, x.dtype),
        scratch_shapes=[pltpu.VMEM(((tm + HALO) * nsl, LANES), jnp.float32),
                        pltpu.VMEM((tm * nsl, LANES), jnp.float32),
                        pltpu.VMEM((tm, 3 * D_MODEL + GROUP_WIDTH), jnp.float32),
                        pltpu.VMEM((n_relayout, tm, LANES), jnp.float32),
                        pltpu.VMEM((n_relayout, tm, LANES), jnp.float32),
                        pltpu.VMEM((tm, D_MODEL), jnp.bfloat16)],
        compiler_params=_params(2),
        name="tail",
    )(x, mod3, mod3, mod3, row(norm_w), *o_groups, *l_groups, *([w_in_bf] * len(side_cols)),
      conv_w.reshape(CONV_KERNEL, nsl, LANES), conv_b.reshape(nsl, LANES), row(ln_w), row(ln_b),
      w_co, w_ao, w_out)


def _attention_caps():
    qi = np.arange(BLK)[:, None]
    kj = np.arange(BLK)[None, :]
    prev = np.where(kj >= qi, BIG, NEG_INF)
    cur = np.where(kj <= qi, BIG, NEG_INF)
    no_prev = np.full((BLK, BLK), NEG_INF)
    caps = np.stack([np.concatenate([prev, cur], axis=1),
                     np.concatenate([no_prev, cur], axis=1)]).astype(np.float32)
    return jnp.asarray(caps)


def kernel(x, c, positions, norm_w, w_ada, b_ada, w_in, conv_w, conv_b, conv_ln_w, conv_ln_b,
           w_conv_out, q_norm_w, k_norm_w, w_attn_out, w_out):
    bsz, seq, _ = x.shape
    bf = jnp.bfloat16
    for window, dilation in DILATED_GROUPS:
        assert window // dilation == BLK and seq % (dilation * BLK) == 0
        assert PERM_ROWS % (SUB_ROWS * dilation) == 0
    assert seq % TM_QKV == 0 and seq % TM_TAIL == 0
    assert TM_QKV % PERM_ROWS == 0 and TM_TAIL % PERM_ROWS == 0

    mod3 = _mod_call(c, w_ada, b_ada).reshape(bsz * 3, 1, D_MODEL)

    w_in_bf = w_in.astype(bf)
    inv_freq = ROPE_THETA ** (-jnp.arange(0, HEAD_DIM, 2, dtype=jnp.float32) / HEAD_DIM)
    freq2 = jnp.concatenate([inv_freq, inv_freq]).reshape(1, HEAD_DIM)
    posf = jnp.broadcast_to(positions.astype(jnp.float32)[..., None], (bsz, seq, HEAD_DIM))
    qkv = _qkv_call(x, mod3, norm_w, w_in_bf, q_norm_w, k_norm_w, posf, freq2)

    caps = _attention_caps()
    o_groups, l_groups = [], []
    for g, (_, dilation) in enumerate(DILATED_GROUPS):
        o_g, l_g = _attn_call(qkv[g], caps, dilation)
        o_groups.append(o_g)
        l_groups.append(l_g)

    return _tail_call(x, mod3, norm_w, o_groups, l_groups, w_in_bf,
                      conv_w, conv_b, conv_ln_w, conv_ln_b,
                      w_conv_out.astype(bf), w_attn_out.astype(bf), w_out.astype(bf))
```

```python
import functools
import math

import jax
import jax.numpy as jnp
import numpy as np
from jax import lax
from jax.experimental import pallas as pl
from jax.experimental.pallas import tpu as pltpu

D_MODEL = 1024
CONV_KERNEL = 31
HEAD_DIM = 128
HEADS_PER_GROUP = 4
GROUP_WIDTH = HEADS_PER_GROUP * HEAD_DIM
DILATED_GROUPS = ((128, 1), (512, 4), (2048, 16))
N_GROUPS = len(DILATED_GROUPS)
BLK = 128
ROPE_THETA = 10000.0
EPS = 1e-6
NEG_INF = -1e30
BIG = 3.0e38
LOG2E = math.log2(math.e)

OFF_A, OFF_B, OFF_GC = 0, 1024, 2048
OFF_Q, OFF_K, OFF_V = 3072, 4608, 6144
OFF_GA, OFF_MC, OFF_MA = 7680, 8192, 9216

LANES = 128
SUB_ROWS = 16
CHUNKS_PER_BLK = BLK // SUB_ROWS
PERM_ROWS = 256
GATHER_STRIDE = 4
HALO = 32
CONV_CHUNK = 8
SIDE_PIECE = 256
SIDE_LAG = 8
VMEM_LIMIT = 56 * 1024 * 1024

TM_QKV = 512
TM_TAIL = 512
PROJ_AHEAD = 2
ATTN_PAIRS = 32
PAIRS_PER_BODY = 4


def _params(n_axes):
    return pltpu.CompilerParams(dimension_semantics=("arbitrary",) * n_axes,
                                vmem_limit_bytes=VMEM_LIMIT)


def _const_spec(shape):
    nd = len(shape)
    return pl.BlockSpec(shape, lambda *_: (0,) * nd, pipeline_mode=pl.Buffered(1))


def _sigmoid(v):
    return jax.nn.sigmoid(v)


def _silu(v):
    return v * jax.nn.sigmoid(v)


def _bdot(a, b):
    return jnp.dot(a, b, preferred_element_type=jnp.float32)


def _zero_from(v):
    bits = pltpu.bitcast(v, jnp.uint32)
    return pltpu.bitcast((bits >> 16) >> 16, jnp.float32)


def _mod_kernel(c_ref, w_ref, b_ref, o_ref):
    sc = _silu(c_ref[...])
    o_ref[...] = jnp.dot(sc, w_ref[...], preferred_element_type=jnp.float32,
                         precision=lax.Precision.HIGHEST) + b_ref[...]


def _mod_call(c, w_ada, b_ada):
    bsz = c.shape[0]
    n_out = w_ada.shape[1]
    bn = 1024
    return pl.pallas_call(
        _mod_kernel,
        grid=(n_out // bn,),
        in_specs=[pl.BlockSpec((bsz, D_MODEL), lambda j: (0, 0)),
                  pl.BlockSpec((D_MODEL, bn), lambda j: (0, j)),
                  pl.BlockSpec((1, bn), lambda j: (0, j))],
        out_specs=pl.BlockSpec((bsz, bn), lambda j: (0, j)),
        out_shape=jax.ShapeDtypeStruct((bsz, n_out), jnp.float32),
        compiler_params=_params(1),
        name="mod",
    )(c, w_ada, b_ada.reshape(1, n_out))


def _normed_input(x, norm_w, scale, shift):
    ms = jnp.mean(x * x, axis=-1, keepdims=True)
    return x * lax.rsqrt(ms + EPS) * norm_w * (1.0 + scale) + shift


def _grouped_shape(bsz, seq, dilation, width):
    return (bsz, seq // (SUB_ROWS * dilation), dilation, SUB_ROWS, width)


def _grouped_tile_spec(tm, dilation, width):
    return pl.BlockSpec((1, tm // (SUB_ROWS * dilation), dilation, SUB_ROWS, width),
                        lambda b, s: (b, s, 0, 0, 0))


def _residue_major(val, dilation, buf_a, buf_b):
    tm = val.shape[0]
    if dilation == 1:
        return val
    buf_a[...] = val
    if dilation == GATHER_STRIDE:
        chunk = SUB_ROWS * dilation
        return jnp.concatenate(
            [buf_a[pl.ds(a * chunk + r, SUB_ROWS, stride=dilation), :]
             for a in range(tm // chunk) for r in range(dilation)], axis=0)
    assert dilation == GATHER_STRIDE * GATHER_STRIDE and PERM_ROWS == SUB_ROWS * dilation
    quarter = PERM_ROWS // GATHER_STRIDE
    for blk in range(tm // PERM_ROWS):
        for r0 in range(GATHER_STRIDE):
            buf_b[blk * PERM_ROWS + r0 * quarter:blk * PERM_ROWS + (r0 + 1) * quarter, :] = (
                buf_a[pl.ds(blk * PERM_ROWS + r0, quarter, stride=GATHER_STRIDE), :])
    return jnp.concatenate(
        [buf_b[pl.ds(blk * PERM_ROWS + (r % GATHER_STRIDE) * quarter + r // GATHER_STRIDE, SUB_ROWS,
                     stride=GATHER_STRIDE), :]
         for blk in range(tm // PERM_ROWS) for r in range(dilation)], axis=0)


def _token_order(val, dilation, buf_a, buf_b):
    tm = val.shape[0]
    if dilation == GATHER_STRIDE:
        chunk = SUB_ROWS * dilation
        for a in range(tm // chunk):
            for r in range(dilation):
                lo = a * chunk + r * SUB_ROWS
                buf_a[pl.ds(a * chunk + r, SUB_ROWS, stride=dilation), :] = val[lo:lo + SUB_ROWS]
        return buf_a[...]
    assert dilation == GATHER_STRIDE * GATHER_STRIDE and PERM_ROWS == SUB_ROWS * dilation
    quarter = PERM_ROWS // GATHER_STRIDE
    for blk in range(tm // PERM_ROWS):
        for r in range(dilation):
            lo = blk * PERM_ROWS + r * SUB_ROWS
            buf_b[pl.ds(blk * PERM_ROWS + (r % GATHER_STRIDE) * quarter + r // GATHER_STRIDE, SUB_ROWS,
                        stride=GATHER_STRIDE), :] = val[lo:lo + SUB_ROWS]
    for blk in range(tm // PERM_ROWS):
        for r0 in range(GATHER_STRIDE):
            lo = blk * PERM_ROWS + r0 * quarter
            buf_a[pl.ds(blk * PERM_ROWS + r0, quarter, stride=GATHER_STRIDE), :] = buf_b[lo:lo + quarter, :]
    return buf_a[...]


def _qkv_kernel(x_ref, shift_ref, scale_ref, nw_ref, wq_ref, wk_ref, wv_ref, qnw_ref, knw_ref, pos_ref,
                freq_ref, *out_and_scratch):
    *out_refs, z_ref, buf_a_ref, buf_b_ref, h_ref = out_and_scratch
    w_refs = (wq_ref, wk_ref, wv_ref)
    x = x_ref[0]
    tm = x.shape[0]
    h_ref[...] = _normed_input(x, nw_ref[...], scale_ref[0], shift_ref[0]).astype(jnp.bfloat16)
    half_rows = tm // 2
    half = HEAD_DIM // 2
    pos = pos_ref[0]
    lane = lax.broadcasted_iota(jnp.int32, (half_rows, HEAD_DIM), 1)
    low = lane < half
    ang = jnp.where(low, pos[:half_rows], pos[half_rows:]) * freq_ref[...]
    cos_p, sin_p = jnp.cos(ang), jnp.sin(ang)
    cos_s, sin_s = pltpu.roll(cos_p, half, axis=1), pltpu.roll(sin_p, half, axis=1)
    cos_t = jnp.concatenate([jnp.where(low, cos_p, cos_s), jnp.where(low, cos_s, cos_p)], axis=0)
    sin_t = jnp.concatenate([jnp.where(low, -sin_p, sin_s), jnp.where(low, -sin_s, sin_p)], axis=0)
    norm_ws = (qnw_ref[...] * (LOG2E * HEAD_DIM ** -0.5), knw_ref[...])
    n_proj = 3 * N_GROUPS

    def project(idx):
        kind, g = divmod(idx, N_GROUPS)
        z_ref[idx] = _bdot(h_ref[...], w_refs[kind][:, g * GROUP_WIDTH:(g + 1) * GROUP_WIDTH])

    for idx in range(PROJ_AHEAD):
        project(idx)
    for idx in range(n_proj):
        kind, g = divmod(idx, N_GROUPS)
        dilation = DILATED_GROUPS[g][1]
        heads = []
        for j in range(HEADS_PER_GROUP):
            zh = z_ref[idx, :, j * HEAD_DIM:(j + 1) * HEAD_DIM]
            if kind < 2:
                ms = jnp.mean(zh * zh, axis=-1, keepdims=True)
                zn = zh * lax.rsqrt(ms + EPS) * norm_ws[kind]
                zh = zn * cos_t + pltpu.roll(zn, half, axis=1) * sin_t
            heads.append(_residue_major(zh, dilation, buf_a_ref.at[j], buf_b_ref.at[j]).astype(jnp.bfloat16))
        o_ref = out_refs[g]
        c0 = kind * GROUP_WIDTH
        o_ref[0, :, :, :, c0:c0 + GROUP_WIDTH] = (
            jnp.concatenate(heads, axis=-1).reshape(o_ref.shape[1:-1] + (GROUP_WIDTH,)))
        if idx + PROJ_AHEAD < n_proj:
            done = pltpu.bitcast(o_ref[0, 0, 0, :, c0:c0 + LANES], jnp.uint32)
            words = pltpu.bitcast(h_ref[0:SUB_ROWS, 0:LANES], jnp.uint32) + ((done >> 16) >> 16)
            h_ref[0:SUB_ROWS, 0:LANES] = pltpu.bitcast(words, jnp.bfloat16)
            project(idx + PROJ_AHEAD)


def _weight_cols_spec(offset, width):
    assert offset % width == 0
    return pl.BlockSpec((D_MODEL, width), lambda *_: (0, offset // width), pipeline_mode=pl.Buffered(1))


def _qkv_call(x, mod3, norm_w, w_in_bf, q_norm_w, k_norm_w, posf, freq2):
    bsz, seq, _ = x.shape
    tm = TM_QKV
    tok = lambda b, s: (b, s, 0)
    dils = [d for _, d in DILATED_GROUPS]
    qkv_width = N_GROUPS * GROUP_WIDTH
    return pl.pallas_call(
        _qkv_kernel,
        grid=(bsz, seq // tm),
        in_specs=[pl.BlockSpec((1, tm, D_MODEL), tok),
                  pl.BlockSpec((1, 1, D_MODEL), lambda b, s: (3 * b, 0, 0)),
                  pl.BlockSpec((1, 1, D_MODEL), lambda b, s: (3 * b + 1, 0, 0)),
                  _const_spec((1, D_MODEL)),
                  _weight_cols_spec(OFF_Q, qkv_width),
                  _weight_cols_spec(OFF_K, qkv_width),
                  _weight_cols_spec(OFF_V, qkv_width),
                  _const_spec((1, HEAD_DIM)),
                  _const_spec((1, HEAD_DIM)),
                  pl.BlockSpec((1, tm, HEAD_DIM), tok),
                  _const_spec((1, HEAD_DIM))],
        out_specs=[_grouped_tile_spec(tm, d, qkv_width) for d in dils],
        out_shape=[jax.ShapeDtypeStruct(_grouped_shape(bsz, seq, d, qkv_width), jnp.bfloat16)
                   for d in dils],
        scratch_shapes=[pltpu.VMEM((3 * N_GROUPS, tm, GROUP_WIDTH), jnp.float32),
                        pltpu.VMEM((HEADS_PER_GROUP, tm, HEAD_DIM), jnp.float32),
                        pltpu.VMEM((HEADS_PER_GROUP, tm, HEAD_DIM), jnp.float32),
                        pltpu.VMEM((tm, D_MODEL), jnp.bfloat16)],
        compiler_params=_params(2),
        name="qkv",
    )(x, mod3, mod3, norm_w.reshape(1, D_MODEL), w_in_bf, w_in_bf, w_in_bf,
      q_norm_w.reshape(1, HEAD_DIM), k_norm_w.reshape(1, HEAD_DIM), posf, freq2)


def _attn_kernel(q_ref, k_ref, v_ref, kp_ref, vp_ref, cap_ref, o_ref, l_ref, *, nb_step, res_step):
    no_prev = (pl.program_id(1) == 0).astype(jnp.int32)
    ones = jnp.ones((BLK, HEAD_DIM), jnp.bfloat16)
    lane = lax.broadcasted_iota(jnp.int32, (BLK, HEAD_DIM), 1)
    nt_dims = (((1,), (1,)), ((), ()))

    def tile(ref, chunk0, r, j):
        t = ref[0, pl.ds(chunk0, CHUNKS_PER_BLK), r, :, j * HEAD_DIM:(j + 1) * HEAD_DIM]
        return t.reshape(BLK, HEAD_DIM)

    def process(pairs, first):
        units = [(i, r, j) for i, r in pairs for j in range(HEADS_PER_GROUP)]
        cap = cap_ref[no_prev] if first else cap_ref[0]
        q = [tile(q_ref, i * CHUNKS_PER_BLK, r, j) for i, r, j in units]
        kc = [tile(k_ref, i * CHUNKS_PER_BLK, r, j) for i, r, j in units]
        vc = [tile(v_ref, i * CHUNKS_PER_BLK, r, j) for i, r, j in units]
        if first:
            kp = [tile(kp_ref, 0, r, j) for _, r, j in units]
            vp = [tile(vp_ref, 0, r, j) for _, r, j in units]
        else:
            kp = [tile(k_ref, (i - 1) * CHUNKS_PER_BLK, r, j) for i, r, j in units]
            vp = [tile(v_ref, (i - 1) * CHUNKS_PER_BLK, r, j) for i, r, j in units]
        s = [lax.dot_general(a, jnp.concatenate([b, c], axis=0), nt_dims, preferred_element_type=jnp.float32)
             for a, b, c in zip(q, kp, kc)]
        s = [jnp.minimum(t, cap) for t in s]
        m = [jnp.max(t, axis=-1, keepdims=True) for t in s]
        p = [jnp.exp2(t - mm).astype(jnp.bfloat16) for t, mm in zip(s, m)]
        oe = [_bdot(t, jnp.concatenate([jnp.concatenate([va, ones], axis=-1),
                                        jnp.concatenate([vb, ones], axis=-1)], axis=0))
              for t, va, vb in zip(p, vp, vc)]
        for n, (i, r) in enumerate(pairs):
            chunks = pl.ds(i * CHUNKS_PER_BLK, CHUNKS_PER_BLK)
            stats = jnp.zeros((BLK, HEAD_DIM), jnp.float32)
            for j in range(HEADS_PER_GROUP):
                u = n * HEADS_PER_GROUP + j
                o_ref[0, chunks, r, :, j * HEAD_DIM:(j + 1) * HEAD_DIM] = (
                    oe[u][:, :HEAD_DIM].astype(o_ref.dtype).reshape(CHUNKS_PER_BLK, SUB_ROWS, HEAD_DIM))
                stats = jnp.where(lane == j, m[u], stats)
                stats = jnp.where(lane == HEADS_PER_GROUP + j, oe[u][:, HEAD_DIM:], stats)
            l_ref[0, chunks, r, :, :] = stats.reshape(CHUNKS_PER_BLK, SUB_ROWS, HEAD_DIM)

    def run(n_pairs, pair_of, first):
        left = n_pairs % PAIRS_PER_BODY
        if left:
            process([pair_of(t) for t in range(left)], first)
        if n_pairs >= PAIRS_PER_BODY:
            def body(t, carry):
                t0 = left + PAIRS_PER_BODY * t
                process([pair_of(t0 + n) for n in range(PAIRS_PER_BODY)], first)
                return carry
            lax.fori_loop(0, n_pairs // PAIRS_PER_BODY, body, 0)

    shift = res_step.bit_length() - 1

    def later_pair(t):
        return 1 + (t >> shift), t & (res_step - 1)

    run(res_step, lambda t: (0, t), True)
    run((nb_step - 1) * res_step, later_pair, False)


def _attn_call(qkv, caps, dilation):
    bsz, n_chunks = qkv.shape[:2]
    nb = n_chunks // CHUNKS_PER_BLK
    nb_step = min(nb, ATTN_PAIRS)
    res_step = min(ATTN_PAIRS // nb_step, dilation)
    assert res_step & (res_step - 1) == 0 and nb % nb_step == 0 and dilation % res_step == 0

    def cur(width, col=0):
        return pl.BlockSpec((1, nb_step * CHUNKS_PER_BLK, res_step, SUB_ROWS, width),
                            lambda b, n, rc: (b, n, rc, 0, col))

    def prev(col):
        return pl.BlockSpec((1, CHUNKS_PER_BLK, res_step, SUB_ROWS, GROUP_WIDTH),
                            lambda b, n, rc: (b, jnp.maximum(n * nb_step - 1, 0), rc, 0, col))

    group_shape = qkv.shape[:-1]
    return pl.pallas_call(
        functools.partial(_attn_kernel, nb_step=nb_step, res_step=res_step),
        grid=(bsz, nb // nb_step, dilation // res_step),
        in_specs=[cur(GROUP_WIDTH, 0), cur(GROUP_WIDTH, 1), cur(GROUP_WIDTH, 2), prev(1), prev(2),
                  _const_spec(caps.shape)],
        out_specs=[cur(GROUP_WIDTH), cur(HEAD_DIM)],
        out_shape=[jax.ShapeDtypeStruct(group_shape + (GROUP_WIDTH,), jnp.bfloat16),
                   jax.ShapeDtypeStruct(group_shape + (HEAD_DIM,), jnp.float32)],
        compiler_params=_params(3),
        name=f"attn_d{dilation}",
    )(qkv, qkv, qkv, qkv, qkv, caps)


def _tail_kernel(x_ref, shift_ref, scale_ref, gate_ref, nw_ref,
                 o0_ref, o1_ref, o2_ref, l0_ref, l1_ref, l2_ref,
                 wa_ref, wb_ref, wgc_ref, wga_ref, wmc_ref, wma_ref,
                 cw_ref, cb_ref, lnw_ref, lnb_ref, wco_ref, wao_ref, wout_ref,
                 out_ref, useq_ref, cseq_ref, side_ref, tok_a_ref, tok_b_ref, tc_ref):
    tm = x_ref.shape[1]
    nsl = D_MODEL // LANES

    @pl.when(pl.program_id(1) == 0)
    def _():
        useq_ref[0:HALO * nsl, :] = jnp.zeros((HALO * nsl, LANES), jnp.float32)

    x = x_ref[0]
    h = _normed_input(x, nw_ref[...], scale_ref[0], shift_ref[0]).astype(jnp.bfloat16)

    for c0 in range(0, D_MODEL, SIDE_PIECE):
        u = _bdot(h, wa_ref[:, c0:c0 + SIDE_PIECE]) * _sigmoid(_bdot(h, wb_ref[:, c0:c0 + SIDE_PIECE]))
        for s in range(SIDE_PIECE // LANES):
            row0 = HALO * nsl + c0 // LANES + s
            useq_ref[pl.ds(row0, tm, stride=nsl), :] = u[:, s * LANES:(s + 1) * LANES]
    def side_piece(w_ref, c0, dst):
        side_ref[:, dst:dst + SIDE_PIECE] = _bdot(h, w_ref[:, c0:c0 + SIDE_PIECE])
        return lambda: side_ref[0:nsl, dst:dst + LANES]

    early_jobs, late_jobs, col = [], [], 0
    for w_ref, early in ((wgc_ref, True), (wga_ref, True), (wmc_ref, False), (wma_ref, False)):
        for c0 in range(0, w_ref.shape[1], SIDE_PIECE):
            (early_jobs if early else late_jobs).append(
                functools.partial(side_piece, w_ref, c0, col + c0))
        col += w_ref.shape[1]
    n_chunks = tm // CONV_CHUNK
    parked = {}
    for c in range(n_chunks):
        base = c * CONV_CHUNK * nsl
        bias = cb_ref[...]
        for read_back in parked.get(c - SIDE_LAG, ()):
            bias = bias + _zero_from(read_back())
        acc = jnp.broadcast_to(bias[None], (CONV_CHUNK, nsl, LANES))
        for t in range(CONV_KERNEL):
            start = base + (HALO - (CONV_KERNEL - 1) + t) * nsl
            taps = useq_ref[pl.ds(start, CONV_CHUNK * nsl), :].reshape(CONV_CHUNK, nsl, LANES)
            acc = acc + cw_ref[t][None] * taps
        cseq_ref[pl.ds(base, CONV_CHUNK * nsl), :] = acc.reshape(CONV_CHUNK * nsl, LANES)
        lo, hi = len(early_jobs) * c // n_chunks, len(early_jobs) * (c + 1) // n_chunks
        parked[c] = [job() for job in early_jobs[lo:hi]]
    late_reads = [job() for job in late_jobs]
    z_gc = side_ref[:, 0:D_MODEL]
    z_ga = side_ref[:, D_MODEL:D_MODEL + GROUP_WIDTH]
    z_mc = side_ref[:, D_MODEL + GROUP_WIDTH:2 * D_MODEL + GROUP_WIDTH]
    z_ma = side_ref[:, 2 * D_MODEL + GROUP_WIDTH:3 * D_MODEL + GROUP_WIDTH]
    useq_ref[0:HALO * nsl, :] = useq_ref[tm * nsl:(tm + HALO) * nsl, :]
    acc = jnp.concatenate([cseq_ref[pl.ds(s, tm, stride=nsl), :] for s in range(nsl)], axis=-1)
    mu = jnp.mean(acc, axis=-1, keepdims=True)
    cen = acc - mu
    var = jnp.mean(cen * cen, axis=-1, keepdims=True)
    ln = cen * lax.rsqrt(var + EPS) * lnw_ref[...] + lnb_ref[...]
    tc_ref[...] = (_silu(ln) * _silu(z_gc)).astype(jnp.bfloat16)
    words = pltpu.bitcast(tc_ref[0:SUB_ROWS, 0:LANES], jnp.uint32)
    for read_back in late_reads:
        words = words + ((pltpu.bitcast(read_back(), jnp.uint32) >> 16) >> 16)
    tc_ref[0:SUB_ROWS, 0:LANES] = pltpu.bitcast(words, jnp.bfloat16)
    y_conv = _bdot(tc_ref[...], wco_ref[...])

    o_g, stats, slot = [], [], 0
    for g, (o_ref, l_ref) in enumerate(((o0_ref, l0_ref), (o1_ref, l1_ref), (o2_ref, l2_ref))):
        dilation = DILATED_GROUPS[g][1]
        o_rows = o_ref[0].reshape(tm, GROUP_WIDTH).astype(jnp.float32)
        slabs = [o_rows[:, j * HEAD_DIM:(j + 1) * HEAD_DIM] for j in range(HEADS_PER_GROUP)]
        slabs.append(l_ref[0].reshape(tm, HEAD_DIM))
        if dilation > 1:
            for n, slab in enumerate(slabs):
                slabs[n] = _token_order(slab, dilation, tok_a_ref.at[slot], tok_b_ref.at[slot])
                slot += 1
        o_g.append(jnp.concatenate(slabs[:HEADS_PER_GROUP], axis=-1))
        stats.append(slabs[HEADS_PER_GROUP])
    top = jnp.maximum(jnp.maximum(stats[0], stats[1]), stats[2])
    e_g = [jnp.exp2(st - top) for st in stats]
    dens = [pltpu.roll(st, LANES - HEADS_PER_GROUP, axis=1) for st in stats]
    total = e_g[0] * dens[0] + e_g[1] * dens[1] + e_g[2] * dens[2]
    head_lane = lax.broadcasted_iota(jnp.int32, total.shape, 1) < HEADS_PER_GROUP
    inv = 1.0 / jnp.where(head_lane, total, 1.0)
    w_g = [e * inv for e in e_g]
    heads = []
    for j in range(HEADS_PER_GROUP):
        cols = slice(j * HEAD_DIM, (j + 1) * HEAD_DIM)
        heads.append(sum(w[:, j:j + 1] * o[:, cols] for w, o in zip(w_g, o_g)))
    o = jnp.concatenate(heads, axis=-1)
    ta = (o * _silu(z_ga)).astype(jnp.bfloat16)
    y_attn = _bdot(ta, wao_ref[...])

    y = _sigmoid(z_mc) * y_conv + _sigmoid(z_ma) * y_attn
    out = _bdot(y.astype(jnp.bfloat16), wout_ref[...])
    out_ref[0] = x + gate_ref[0] * out


def _tail_call(x, mod3, norm_w, o_groups, l_groups, w_in_bf, conv_w, conv_b, ln_w, ln_b,
               w_co, w_ao, w_out):
    side_cols = ((OFF_A, D_MODEL), (OFF_B, D_MODEL), (OFF_GC, D_MODEL),
                 (OFF_GA, GROUP_WIDTH), (OFF_MC, D_MODEL), (OFF_MA, D_MODEL))
    bsz, seq, _ = x.shape
    tm = TM_TAIL
    nsl = D_MODEL // LANES
    tok = lambda b, s: (b, s, 0)
    row = lambda v: v.reshape(1, D_MODEL)
    modspec = lambda k: pl.BlockSpec((1, 1, D_MODEL), lambda b, s: (3 * b + k, 0, 0))
    dils = [d for _, d in DILATED_GROUPS]
    n_relayout = (HEADS_PER_GROUP + 1) * sum(d > 1 for d in dils)
    in_specs = ([pl.BlockSpec((1, tm, D_MODEL), tok), modspec(0), modspec(1), modspec(2),
                 _const_spec((1, D_MODEL))]
                + [_grouped_tile_spec(tm, d, GROUP_WIDTH) for d in dils]
                + [_grouped_tile_spec(tm, d, HEAD_DIM) for d in dils]
                + [_weight_cols_spec(off, width) for off, width in side_cols]
                + [_const_spec((CONV_KERNEL, nsl, LANES)), _const_spec((nsl, LANES))]
                + [_const_spec((1, D_MODEL))] * 2
                + [_const_spec(w_co.shape), _const_spec(w_ao.shape), _const_spec(w_out.shape)])
    return pl.pallas_call(
        _tail_kernel,
        grid=(bsz, seq // tm),
        in_specs=in_specs,
        out_specs=pl.BlockSpec((1, tm, D_MODEL), tok),
        out_shape=jax.ShapeDtypeStruct(x.shape, x.dtype),
        scratch_shapes=[pltpu.VMEM(((tm + HALO) * nsl, LANES), jnp.float32),
                        pltpu.VMEM((tm * nsl, LANES), jnp.float32),
                        pltpu.VMEM((tm, 3 * D_MODEL + GROUP_WIDTH), jnp.float32),
                        pltpu.VMEM((n_relayout, tm, LANES), jnp.float32),
                        pltpu.VMEM((n_relayout, tm, LANES), jnp.float32),
                        pltpu.VMEM((tm, D_MODEL), jnp.bfloat16)],
        compiler_params=_params(2),
        name="tail",
    )(x, mod3, mod3, mod3, row(norm_w), *o_groups, *l_groups, *([w_in_bf] * len(side_cols)),
      conv_w.reshape(CONV_KERNEL, nsl, LANES), conv_b.reshape(nsl, LANES), row(ln_w), row(ln_b),
      w_co, w_ao, w_out)


def _attention_caps():
    qi = np.arange(BLK)[:, None]
    kj = np.arange(BLK)[None, :]
    prev = np.where(kj >= qi, BIG, NEG_INF)
    cur = np.where(kj <= qi, BIG, NEG_INF)
    no_prev = np.full((BLK, BLK), NEG_INF)
    caps = np.stack([np.concatenate([prev, cur], axis=1),
                     np.concatenate([no_prev, cur], axis=1)]).astype(np.float32)
    return jnp.asarray(caps)


def kernel(x, c, positions, norm_w, w_ada, b_ada, w_in, conv_w, conv_b, conv_ln_w, conv_ln_b,
           w_conv_out, q_norm_w, k_norm_w, w_attn_out, w_out):
    bsz, seq, _ = x.shape
    bf = jnp.bfloat16
    for window, dilation in DILATED_GROUPS:
        assert window // dilation == BLK and seq % (dilation * BLK) == 0
        assert PERM_ROWS % (SUB_ROWS * dilation) == 0
    assert seq % TM_QKV == 0 and seq % TM_TAIL == 0
    assert TM_QKV % PERM_ROWS == 0 and TM_TAIL % PERM_ROWS == 0

    mod3 = _mod_call(c, w_ada, b_ada).reshape(bsz * 3, 1, D_MODEL)

    w_in_bf = w_in.astype(bf)
    inv_freq = ROPE_THETA ** (-jnp.arange(0, HEAD_DIM, 2, dtype=jnp.float32) / HEAD_DIM)
    freq2 = jnp.concatenate([inv_freq, inv_freq]).reshape(1, HEAD_DIM)
    posf = jnp.broadcast_to(positions.astype(jnp.float32)[..., None], (bsz, seq, HEAD_DIM))
    qkv = _qkv_call(x, mod3, norm_w, w_in_bf, q_norm_w, k_norm_w, posf, freq2)

    caps = _attention_caps()
    o_groups, l_groups = [], []
    for g, (_, dilation) in enumerate(DILATED_GROUPS):
        o_g, l_g = _attn_call(qkv[g], caps, dilation)
        o_groups.append(o_g)
        l_groups.append(l_g)

    return _tail_call(x, mod3, norm_w, o_groups, l_groups, w_in_bf,
                      conv_w, conv_b, conv_ln_w, conv_ln_b,
                      w_conv_out.astype(bf), w_attn_out.astype(bf), w_out.astype(bf))
```

```python
import functools
import math

import jax
import jax.numpy as jnp
import numpy as np
from jax import lax
from jax.experimental import pallas as pl
from jax.experimental.pallas import tpu as pltpu

D_MODEL = 1024
CONV_KERNEL = 31
HEAD_DIM = 128
HEADS_PER_GROUP = 4
GROUP_WIDTH = HEADS_PER_GROUP * HEAD_DIM
DILATED_GROUPS = ((128, 1), (512, 4), (2048, 16))
N_GROUPS = len(DILATED_GROUPS)
BLK = 128
ROPE_THETA = 10000.0
EPS = 1e-6
NEG_INF = -1e30
BIG = 3.0e38
LOG2E = math.log2(math.e)

OFF_A, OFF_B, OFF_GC = 0, 1024, 2048
OFF_Q, OFF_K, OFF_V = 3072, 4608, 6144
OFF_GA, OFF_MC, OFF_MA = 7680, 8192, 9216

LANES = 128
SUB_ROWS = 16
CHUNKS_PER_BLK = BLK // SUB_ROWS
PERM_ROWS = 256
GATHER_STRIDE = 4
HALO = 32
CONV_CHUNK = 8
SIDE_PIECE = 256
SIDE_LAG = 8
VMEM_LIMIT = 56 * 1024 * 1024

TM_QKV = 512
TM_TAIL = 512
PROJ_AHEAD = 2
ATTN_PAIRS = 32
PAIRS_PER_BODY = 4


def _params(n_axes):
    return pltpu.CompilerParams(dimension_semantics=("arbitrary",) * n_axes,
                                vmem_limit_bytes=VMEM_LIMIT)


def _const_spec(shape):
    nd = len(shape)
    return pl.BlockSpec(shape, lambda *_: (0,) * nd, pipeline_mode=pl.Buffered(1))


def _sigmoid(v):
    return jax.nn.sigmoid(v)


def _silu(v):
    return v * jax.nn.sigmoid(v)


def _bdot(a, b):
    return jnp.dot(a, b, preferred_element_type=jnp.float32)


def _zero_from(v):
    bits = pltpu.bitcast(v, jnp.uint32)
    return pltpu.bitcast((bits >> 16) >> 16, jnp.float32)


def _mod_kernel(c_ref, w_ref, b_ref, o_ref):
    sc = _silu(c_ref[...])
    o_ref[...] = jnp.dot(sc, w_ref[...], preferred_element_type=jnp.float32,
                         precision=lax.Precision.HIGHEST) + b_ref[...]


def _mod_call(c, w_ada, b_ada):
    bsz = c.shape[0]
    n_out = w_ada.shape[1]
    bn = 1024
    return pl.pallas_call(
        _mod_kernel,
        grid=(n_out // bn,),
        in_specs=[pl.BlockSpec((bsz, D_MODEL), lambda j: (0, 0)),
                  pl.BlockSpec((D_MODEL, bn), lambda j: (0, j)),
                  pl.BlockSpec((1, bn), lambda j: (0, j))],
        out_specs=pl.BlockSpec((bsz, bn), lambda j: (0, j)),
        out_shape=jax.ShapeDtypeStruct((bsz, n_out), jnp.float32),
        compiler_params=_params(1),
        name="mod",
    )(c, w_ada, b_ada.reshape(1, n_out))


def _normed_input(x, norm_w, scale, shift):
    ms = jnp.mean(x * x, axis=-1, keepdims=True)
    return x * lax.rsqrt(ms + EPS) * norm_w * (1.0 + scale) + shift


def _grouped_shape(bsz, seq, dilation, width):
    return (bsz, seq // (SUB_ROWS * dilation), dilation, SUB_ROWS, width)


def _grouped_tile_spec(tm, dilation, width):
    return pl.BlockSpec((1, tm // (SUB_ROWS * dilation), dilation, SUB_ROWS, width),
                        lambda b, s: (b, s, 0, 0, 0))


def _residue_major(val, dilation, buf_a, buf_b):
    tm = val.shape[0]
    if dilation == 1:
        return val
    buf_a[...] = val
    if dilation == GATHER_STRIDE:
        chunk = SUB_ROWS * dilation
        return jnp.concatenate(
            [buf_a[pl.ds(a * chunk + r, SUB_ROWS, stride=dilation), :]
             for a in range(tm // chunk) for r in range(dilation)], axis=0)
    assert dilation == GATHER_STRIDE * GATHER_STRIDE and PERM_ROWS == SUB_ROWS * dilation
    quarter = PERM_ROWS // GATHER_STRIDE
    for blk in range(tm // PERM_ROWS):
        for r0 in range(GATHER_STRIDE):
            buf_b[blk * PERM_ROWS + r0 * quarter:blk * PERM_ROWS + (r0 + 1) * quarter, :] = (
                buf_a[pl.ds(blk * PERM_ROWS + r0, quarter, stride=GATHER_STRIDE), :])
    return jnp.concatenate(
        [buf_b[pl.ds(blk * PERM_ROWS + (r % GATHER_STRIDE) * quarter + r // GATHER_STRIDE, SUB_ROWS,
                     stride=GATHER_STRIDE), :]
         for blk in range(tm // PERM_ROWS) for r in range(dilation)], axis=0)


def _token_order(val, dilation, buf_a, buf_b):
    tm = val.shape[0]
    if dilation == GATHER_STRIDE:
        chunk = SUB_ROWS * dilation
        for a in range(tm // chunk):
            for r in range(dilation):
                lo = a * chunk + r * SUB_ROWS
                buf_a[pl.ds(a * chunk + r, SUB_ROWS, stride=dilation), :] = val[lo:lo + SUB_ROWS]
        return buf_a[...]
    assert dilation == GATHER_STRIDE * GATHER_STRIDE and PERM_ROWS == SUB_ROWS * dilation
    quarter = PERM_ROWS // GATHER_STRIDE
    for blk in range(tm // PERM_ROWS):
        for r in range(dilation):
            lo = blk * PERM_ROWS + r * SUB_ROWS
            buf_b[pl.ds(blk * PERM_ROWS + (r % GATHER_STRIDE) * quarter + r // GATHER_STRIDE, SUB_ROWS,
                        stride=GATHER_STRIDE), :] = val[lo:lo + SUB_ROWS]
    for blk in range(tm // PERM_ROWS):
        for r0 in range(GATHER_STRIDE):
            lo = blk * PERM_ROWS + r0 * quarter
            buf_a[pl.ds(blk * PERM_ROWS + r0, quarter, stride=GATHER_STRIDE), :] = buf_b[lo:lo + quarter, :]
    return buf_a[...]


def _qkv_kernel(x_ref, shift_ref, scale_ref, nw_ref, wq_ref, wk_ref, wv_ref, qnw_ref, knw_ref, pos_ref,
                freq_ref, *out_and_scratch):
    *out_refs, z_ref, buf_a_ref, buf_b_ref, h_ref = out_and_scratch
    w_refs = (wq_ref, wk_ref, wv_ref)
    x = x_ref[0]
    tm = x.shape[0]
    h_ref[...] = _normed_input(x, nw_ref[...], scale_ref[0], shift_ref[0]).astype(jnp.bfloat16)
    half_rows = tm // 2
    half = HEAD_DIM // 2
    pos = pos_ref[0]
    lane = lax.broadcasted_iota(jnp.int32, (half_rows, HEAD_DIM), 1)
    low = lane < half
    ang = jnp.where(low, pos[:half_rows], pos[half_rows:]) * freq_ref[...]
    cos_p, sin_p = jnp.cos(ang), jnp.sin(ang)
    cos_s, sin_s = pltpu.roll(cos_p, half, axis=1), pltpu.roll(sin_p, half, axis=1)
    cos_t = jnp.concatenate([jnp.where(low, cos_p, cos_s), jnp.where(low, cos_s, cos_p)], axis=0)
    sin_t = jnp.concatenate([jnp.where(low, -sin_p, sin_s), jnp.where(low, -sin_s, sin_p)], axis=0)
    norm_ws = (qnw_ref[...] * (LOG2E * HEAD_DIM ** -0.5), knw_ref[...])
    n_proj = 3 * N_GROUPS

    def project(idx):
        kind, g = divmod(idx, N_GROUPS)
        z_ref[idx] = _bdot(h_ref[...], w_refs[kind][:, g * GROUP_WIDTH:(g + 1) * GROUP_WIDTH])

    for idx in range(PROJ_AHEAD):
        project(idx)
    for idx in range(n_proj):
        kind, g = divmod(idx, N_GROUPS)
        dilation = DILATED_GROUPS[g][1]
        heads = []
        for j in range(HEADS_PER_GROUP):
            zh = z_ref[idx, :, j * HEAD_DIM:(j + 1) * HEAD_DIM]
            if kind < 2:
                ms = jnp.mean(zh * zh, axis=-1, keepdims=True)
                zn = zh * lax.rsqrt(ms + EPS) * norm_ws[kind]
                zh = zn * cos_t + pltpu.roll(zn, half, axis=1) * sin_t
            heads.append(_residue_major(zh, dilation, buf_a_ref.at[j], buf_b_ref.at[j]).astype(jnp.bfloat16))
        o_ref = out_refs[g]
        c0 = kind * GROUP_WIDTH
        o_ref[0, :, :, :, c0:c0 + GROUP_WIDTH] = (
            jnp.concatenate(heads, axis=-1).reshape(o_ref.shape[1:-1] + (GROUP_WIDTH,)))
        if idx + PROJ_AHEAD < n_proj:
            done = pltpu.bitcast(o_ref[0, 0, 0, :, c0:c0 + LANES], jnp.uint32)
            words = pltpu.bitcast(h_ref[0:SUB_ROWS, 0:LANES], jnp.uint32) + ((done >> 16) >> 16)
            h_ref[0:SUB_ROWS, 0:LANES] = pltpu.bitcast(words, jnp.bfloat16)
            project(idx + PROJ_AHEAD)


def _weight_cols_spec(offset, width):
    assert offset % width == 0
    return pl.BlockSpec((D_MODEL, width), lambda *_: (0, offset // width), pipeline_mode=pl.Buffered(1))


def _qkv_call(x, mod3, norm_w, w_in_bf, q_norm_w, k_norm_w, posf, freq2):
    bsz, seq, _ = x.shape
    tm = TM_QKV
    tok = lambda b, s: (b, s, 0)
    dils = [d for _, d in DILATED_GROUPS]
    qkv_width = N_GROUPS * GROUP_WIDTH
    return pl.pallas_call(
        _qkv_kernel,
        grid=(bsz, seq // tm),
        in_specs=[pl.BlockSpec((1, tm, D_MODEL), tok),
                  pl.BlockSpec((1, 1, D_MODEL), lambda b, s: (3 * b, 0, 0)),
                  pl.BlockSpec((1, 1, D_MODEL), lambda b, s: (3 * b + 1, 0, 0)),
                  _const_spec((1, D_MODEL)),
                  _weight_cols_spec(OFF_Q, qkv_width),
                  _weight_cols_spec(OFF_K, qkv_width),
                  _weight_cols_spec(OFF_V, qkv_width),
                  _const_spec((1, HEAD_DIM)),
                  _const_spec((1, HEAD_DIM)),
                  pl.BlockSpec((1, tm, HEAD_DIM), tok),
                  _const_spec((1, HEAD_DIM))],
        out_specs=[_grouped_tile_spec(tm, d, qkv_width) for d in dils],
        out_shape=[jax.ShapeDtypeStruct(_grouped_shape(bsz, seq, d, qkv_width), jnp.bfloat16)
                   for d in dils],
        scratch_shapes=[pltpu.VMEM((3 * N_GROUPS, tm, GROUP_WIDTH), jnp.float32),
                        pltpu.VMEM((HEADS_PER_GROUP, tm, HEAD_DIM), jnp.float32),
                        pltpu.VMEM((HEADS_PER_GROUP, tm, HEAD_DIM), jnp.float32),
                        pltpu.VMEM((tm, D_MODEL), jnp.bfloat16)],
        compiler_params=_params(2),
        name="qkv",
    )(x, mod3, mod3, norm_w.reshape(1, D_MODEL), w_in_bf, w_in_bf, w_in_bf,
      q_norm_w.reshape(1, HEAD_DIM), k_norm_w.reshape(1, HEAD_DIM), posf, freq2)


def _attn_kernel(q_ref, k_ref, v_ref, kp_ref, vp_ref, cap_ref, o_ref, l_ref, *, nb_step, res_step):
    no_prev = (pl.program_id(1) == 0).astype(jnp.int32)
    ones = jnp.ones((BLK, HEAD_DIM), jnp.bfloat16)
    lane = lax.broadcasted_iota(jnp.int32, (BLK, HEAD_DIM), 1)
    nt_dims = (((1,), (1,)), ((), ()))

    def tile(ref, chunk0, r, j):
        t = ref[0, pl.ds(chunk0, CHUNKS_PER_BLK), r, :, j * HEAD_DIM:(j + 1) * HEAD_DIM]
        return t.reshape(BLK, HEAD_DIM)

    def process(pairs, first):
        units = [(i, r, j) for i, r in pairs for j in range(HEADS_PER_GROUP)]
        cap = cap_ref[no_prev] if first else cap_ref[0]
        q = [tile(q_ref, i * CHUNKS_PER_BLK, r, j) for i, r, j in units]
        kc = [tile(k_ref, i * CHUNKS_PER_BLK, r, j) for i, r, j in units]
        vc = [tile(v_ref, i * CHUNKS_PER_BLK, r, j) for i, r, j in units]
        if first:
            kp = [tile(kp_ref, 0, r, j) for _, r, j in units]
            vp = [tile(vp_ref, 0, r, j) for _, r, j in units]
        else:
            kp = [tile(k_ref, (i - 1) * CHUNKS_PER_BLK, r, j) for i, r, j in units]
            vp = [tile(v_ref, (i - 1) * CHUNKS_PER_BLK, r, j) for i, r, j in units]
        s = [lax.dot_general(a, jnp.concatenate([b, c], axis=0), nt_dims, preferred_element_type=jnp.float32)
             for a, b, c in zip(q, kp, kc)]
        s = [jnp.minimum(t, cap) for t in s]
        m = [jnp.max(t, axis=-1, keepdims=True) for t in s]
        p = [jnp.exp2(t - mm).astype(jnp.bfloat16) for t, mm in zip(s, m)]
        oe = [_bdot(t, jnp.concatenate([jnp.concatenate([va, ones], axis=-1),
                                        jnp.concatenate([vb, ones], axis=-1)], axis=0))
              for t, va, vb in zip(p, vp, vc)]
        for n, (i, r) in enumerate(pairs):
            chunks = pl.ds(i * CHUNKS_PER_BLK, CHUNKS_PER_BLK)
            stats = jnp.zeros((BLK, HEAD_DIM), jnp.float32)
            for j in range(HEADS_PER_GROUP):
                u = n * HEADS_PER_GROUP + j
                o_ref[0, chunks, r, :, j * HEAD_DIM:(j + 1) * HEAD_DIM] = (
                    oe[u][:, :HEAD_DIM].astype(o_ref.dtype).reshape(CHUNKS_PER_BLK, SUB_ROWS, HEAD_DIM))
                stats = jnp.where(lane == j, m[u], stats)
                stats = jnp.where(lane == HEADS_PER_GROUP + j, oe[u][:, HEAD_DIM:], stats)
            l_ref[0, chunks, r, :, :] = stats.reshape(CHUNKS_PER_BLK, SUB_ROWS, HEAD_DIM)

    def run(n_pairs, pair_of, first):
        left = n_pairs % PAIRS_PER_BODY
        if left:
            process([pair_of(t) for t in range(left)], first)
        if n_pairs >= PAIRS_PER_BODY:
            def body(t, carry):
                t0 = left + PAIRS_PER_BODY * t
                process([pair_of(t0 + n) for n in range(PAIRS_PER_BODY)], first)
                return carry
            lax.fori_loop(0, n_pairs // PAIRS_PER_BODY, body, 0)

    shift = res_step.bit_length() - 1

    def later_pair(t):
        return 1 + (t >> shift), t & (res_step - 1)

    run(res_step, lambda t: (0, t), True)
    run((nb_step - 1) * res_step, later_pair, False)


def _attn_call(qkv, caps, dilation):
    bsz, n_chunks = qkv.shape[:2]
    nb = n_chunks // CHUNKS_PER_BLK
    nb_step = min(nb, ATTN_PAIRS)
    res_step = min(ATTN_PAIRS // nb_step, dilation)
    assert res_step & (res_step - 1) == 0 and nb % nb_step == 0 and dilation % res_step == 0

    def cur(width, col=0):
        return pl.BlockSpec((1, nb_step * CHUNKS_PER_BLK, res_step, SUB_ROWS, width),
                            lambda b, n, rc: (b, n, rc, 0, col))

    def prev(col):
        if nb_step == nb:
            return pl.BlockSpec((1, CHUNKS_PER_BLK, res_step, SUB_ROWS, GROUP_WIDTH),
                                lambda b, n, rc: (0, 0, rc, 0, col))
        return pl.BlockSpec((1, CHUNKS_PER_BLK, res_step, SUB_ROWS, GROUP_WIDTH),
                            lambda b, n, rc: (b, jnp.maximum(n * nb_step - 1, 0), rc, 0, col))

    group_shape = qkv.shape[:-1]
    return pl.pallas_call(
        functools.partial(_attn_kernel, nb_step=nb_step, res_step=res_step),
        grid=(bsz, nb // nb_step, dilation // res_step),
        in_specs=[cur(GROUP_WIDTH, 0), cur(GROUP_WIDTH, 1), cur(GROUP_WIDTH, 2), prev(1), prev(2),
                  _const_spec(caps.shape)],
        out_specs=[cur(GROUP_WIDTH), cur(HEAD_DIM)],
        out_shape=[jax.ShapeDtypeStruct(group_shape + (GROUP_WIDTH,), jnp.bfloat16),
                   jax.ShapeDtypeStruct(group_shape + (HEAD_DIM,), jnp.float32)],
        compiler_params=_params(3),
        name=f"attn_d{dilation}",
    )(qkv, qkv, qkv, qkv, qkv, caps)


def _tail_kernel(x_ref, shift_ref, scale_ref, gate_ref, nw_ref,
                 o0_ref, o1_ref, o2_ref, l0_ref, l1_ref, l2_ref,
                 wa_ref, wb_ref, wgc_ref, wga_ref, wmc_ref, wma_ref,
                 cw_ref, cb_ref, lnw_ref, lnb_ref, wco_ref, wao_ref, wout_ref,
                 out_ref, useq_ref, cseq_ref, side_ref, tok_a_ref, tok_b_ref, tc_ref):
    tm = x_ref.shape[1]
    nsl = D_MODEL // LANES

    @pl.when(pl.program_id(1) == 0)
    def _():
        useq_ref[0:HALO * nsl, :] = jnp.zeros((HALO * nsl, LANES), jnp.float32)

    x = x_ref[0]
    h = _normed_input(x, nw_ref[...], scale_ref[0], shift_ref[0]).astype(jnp.bfloat16)

    for c0 in range(0, D_MODEL, SIDE_PIECE):
        u = _bdot(h, wa_ref[:, c0:c0 + SIDE_PIECE]) * _sigmoid(_bdot(h, wb_ref[:, c0:c0 + SIDE_PIECE]))
        for s in range(SIDE_PIECE // LANES):
            row0 = HALO * nsl + c0 // LANES + s
            useq_ref[pl.ds(row0, tm, stride=nsl), :] = u[:, s * LANES:(s + 1) * LANES]
    def side_piece(w_ref, c0, dst):
        side_ref[:, dst:dst + SIDE_PIECE] = _bdot(h, w_ref[:, c0:c0 + SIDE_PIECE])
        return lambda: side_ref[0:nsl, dst:dst + LANES]

    early_jobs, late_jobs, col = [], [], 0
    for w_ref, early in ((wgc_ref, True), (wga_ref, True), (wmc_ref, False), (wma_ref, False)):
        for c0 in range(0, w_ref.shape[1], SIDE_PIECE):
            (early_jobs if early else late_jobs).append(
                functools.partial(side_piece, w_ref, c0, col + c0))
        col += w_ref.shape[1]
    n_chunks = tm // CONV_CHUNK
    parked = {}
    for c in range(n_chunks):
        base = c * CONV_CHUNK * nsl
        bias = cb_ref[...]
        for read_back in parked.get(c - SIDE_LAG, ()):
            bias = bias + _zero_from(read_back())
        acc = jnp.broadcast_to(bias[None], (CONV_CHUNK, nsl, LANES))
        for t in range(CONV_KERNEL):
            start = base + (HALO - (CONV_KERNEL - 1) + t) * nsl
            taps = useq_ref[pl.ds(start, CONV_CHUNK * nsl), :].reshape(CONV_CHUNK, nsl, LANES)
            acc = acc + cw_ref[t][None] * taps
        cseq_ref[pl.ds(base, CONV_CHUNK * nsl), :] = acc.reshape(CONV_CHUNK * nsl, LANES)
        lo, hi = len(early_jobs) * c // n_chunks, len(early_jobs) * (c + 1) // n_chunks
        parked[c] = [job() for job in early_jobs[lo:hi]]
    late_reads = [job() for job in late_jobs]
    z_gc = side_ref[:, 0:D_MODEL]
    z_ga = side_ref[:, D_MODEL:D_MODEL + GROUP_WIDTH]
    z_mc = side_ref[:, D_MODEL + GROUP_WIDTH:2 * D_MODEL + GROUP_WIDTH]
    z_ma = side_ref[:, 2 * D_MODEL + GROUP_WIDTH:3 * D_MODEL + GROUP_WIDTH]
    useq_ref[0:HALO * nsl, :] = useq_ref[tm * nsl:(tm + HALO) * nsl, :]
    acc = jnp.concatenate([cseq_ref[pl.ds(s, tm, stride=nsl), :] for s in range(nsl)], axis=-1)
    mu = jnp.mean(acc, axis=-1, keepdims=True)
    cen = acc - mu
    var = jnp.mean(cen * cen, axis=-1, keepdims=True)
    ln = cen * lax.rsqrt(var + EPS) * lnw_ref[...] + lnb_ref[...]
    tc_ref[...] = (_silu(ln) * _silu(z_gc)).astype(jnp.bfloat16)
    words = pltpu.bitcast(tc_ref[0:SUB_ROWS, 0:LANES], jnp.uint32)
    for read_back in late_reads:
        words = words + ((pltpu.bitcast(read_back(), jnp.uint32) >> 16) >> 16)
    tc_ref[0:SUB_ROWS, 0:LANES] = pltpu.bitcast(words, jnp.bfloat16)
    y_conv = _bdot(tc_ref[...], wco_ref[...])

    o_g, stats, slot = [], [], 0
    for g, (o_ref, l_ref) in enumerate(((o0_ref, l0_ref), (o1_ref, l1_ref), (o2_ref, l2_ref))):
        dilation = DILATED_GROUPS[g][1]
        o_rows = o_ref[0].reshape(tm, GROUP_WIDTH).astype(jnp.float32)
        slabs = [o_rows[:, j * HEAD_DIM:(j + 1) * HEAD_DIM] for j in range(HEADS_PER_GROUP)]
        slabs.append(l_ref[0].reshape(tm, HEAD_DIM))
        if dilation > 1:
            for n, slab in enumerate(slabs):
                slabs[n] = _token_order(slab, dilation, tok_a_ref.at[slot], tok_b_ref.at[slot])
                slot += 1
        o_g.append(jnp.concatenate(slabs[:HEADS_PER_GROUP], axis=-1))
        stats.append(slabs[HEADS_PER_GROUP])
    top = jnp.maximum(jnp.maximum(stats[0], stats[1]), stats[2])
    e_g = [jnp.exp2(st - top) for st in stats]
    dens = [pltpu.roll(st, LANES - HEADS_PER_GROUP, axis=1) for st in stats]
    total = e_g[0] * dens[0] + e_g[1] * dens[1] + e_g[2] * dens[2]
    head_lane = lax.broadcasted_iota(jnp.int32, total.shape, 1) < HEADS_PER_GROUP
    inv = 1.0 / jnp.where(head_lane, total, 1.0)
    w_g = [e * inv for e in e_g]
    heads = []
    for j in range(HEADS_PER_GROUP):
        cols = slice(j * HEAD_DIM, (j + 1) * HEAD_DIM)
        heads.append(sum(w[:, j:j + 1] * o[:, cols] for w, o in zip(w_g, o_g)))
    o = jnp.concatenate(heads, axis=-1)
    ta = (o * _silu(z_ga)).astype(jnp.bfloat16)
    y_attn = _bdot(ta, wao_ref[...])

    y = _sigmoid(z_mc) * y_conv + _sigmoid(z_ma) * y_attn
    out = _bdot(y.astype(jnp.bfloat16), wout_ref[...])
    out_ref[0] = x + gate_ref[0] * out


def _tail_call(x, mod3, norm_w, o_groups, l_groups, w_in_bf, conv_w, conv_b, ln_w, ln_b,
               w_co, w_ao, w_out):
    side_cols = ((OFF_A, D_MODEL), (OFF_B, D_MODEL), (OFF_GC, D_MODEL),
                 (OFF_GA, GROUP_WIDTH), (OFF_MC, D_MODEL), (OFF_MA, D_MODEL))
    bsz, seq, _ = x.shape
    tm = TM_TAIL
    nsl = D_MODEL // LANES
    tok = lambda b, s: (b, s, 0)
    row = lambda v: v.reshape(1, D_MODEL)
    modspec = lambda k: pl.BlockSpec((1, 1, D_MODEL), lambda b, s: (3 * b + k, 0, 0))
    dils = [d for _, d in DILATED_GROUPS]
    n_relayout = (HEADS_PER_GROUP + 1) * sum(d > 1 for d in dils)
    in_specs = ([pl.BlockSpec((1, tm, D_MODEL), tok), modspec(0), modspec(1), modspec(2),
                 _const_spec((1, D_MODEL))]
                + [_grouped_tile_spec(tm, d, GROUP_WIDTH) for d in dils]
                + [_grouped_tile_spec(tm, d, HEAD_DIM) for d in dils]
                + [_weight_cols_spec(off, width) for off, width in side_cols]
                + [_const_spec((CONV_KERNEL, nsl, LANES)), _const_spec((nsl, LANES))]
                + [_const_spec((1, D_MODEL))] * 2
                + [_const_spec(w_co.shape), _const_spec(w_ao.shape), _const_spec(w_out.shape)])
    return pl.pallas_call(
        _tail_kernel,
        grid=(bsz, seq // tm),
        in_specs=in_specs,
        out_specs=pl.BlockSpec((1, tm, D_MODEL), tok),
        out_shape=jax.ShapeDtypeStruct(x.shape, x.dtype),
        scratch_shapes=[pltpu.VMEM(((tm + HALO) * nsl, LANES), jnp.float32),
                        pltpu.VMEM((tm * nsl, LANES), jnp.float32),
                        pltpu.VMEM((tm, 3 * D_MODEL + GROUP_WIDTH), jnp.float32),
                        pltpu.VMEM((n_relayout, tm, LANES), jnp.float32),
                        pltpu.VMEM((n_relayout, tm, LANES), jnp.float32),
                        pltpu.VMEM((tm, D_MODEL), jnp.bfloat16)],
        compiler_params=_params(2),
        name="tail",
    )(x, mod3, mod3, mod3, row(norm_w), *o_groups, *l_groups, *([w_in_bf] * len(side_cols)),
      conv_w.reshape(CONV_KERNEL, nsl, LANES), conv_b.reshape(nsl, LANES), row(ln_w), row(ln_b),
      w_co, w_ao, w_out)


def _attention_caps():
    qi = np.arange(BLK)[:, None]
    kj = np.arange(BLK)[None, :]
    prev = np.where(kj >= qi, BIG, NEG_INF)
    cur = np.where(kj <= qi, BIG, NEG_INF)
    no_prev = np.full((BLK, BLK), NEG_INF)
    caps = np.stack([np.concatenate([prev, cur], axis=1),
                     np.concatenate([no_prev, cur], axis=1)]).astype(np.float32)
    return jnp.asarray(caps)


def kernel(x, c, positions, norm_w, w_ada, b_ada, w_in, conv_w, conv_b, conv_ln_w, conv_ln_b,
           w_conv_out, q_norm_w, k_norm_w, w_attn_out, w_out):
    bsz, seq, _ = x.shape
    bf = jnp.bfloat16
    for window, dilation in DILATED_GROUPS:
        assert window // dilation == BLK and seq % (dilation * BLK) == 0
        assert PERM_ROWS % (SUB_ROWS * dilation) == 0
    assert seq % TM_QKV == 0 and seq % TM_TAIL == 0
    assert TM_QKV % PERM_ROWS == 0 and TM_TAIL % PERM_ROWS == 0

    mod3 = _mod_call(c, w_ada, b_ada).reshape(bsz * 3, 1, D_MODEL)

    w_in_bf = w_in.astype(bf)
    inv_freq = ROPE_THETA ** (-jnp.arange(0, HEAD_DIM, 2, dtype=jnp.float32) / HEAD_DIM)
    freq2 = jnp.concatenate([inv_freq, inv_freq]).reshape(1, HEAD_DIM)
    posf = jnp.broadcast_to(positions.astype(jnp.float32)[..., None], (bsz, seq, HEAD_DIM))
    qkv = _qkv_call(x, mod3, norm_w, w_in_bf, q_norm_w, k_norm_w, posf, freq2)

    caps = _attention_caps()
    o_groups, l_groups = [], []
    for g, (_, dilation) in enumerate(DILATED_GROUPS):
        o_g, l_g = _attn_call(qkv[g], caps, dilation)
        o_groups.append(o_g)
        l_groups.append(l_g)

    return _tail_call(x, mod3, norm_w, o_groups, l_groups, w_in_bf,
                      conv_w, conv_b, conv_ln_w, conv_ln_b,
                      w_conv_out.astype(bf), w_attn_out.astype(bf), w_out.astype(bf))
```

```python
import functools
import math

import jax
import jax.numpy as jnp
import numpy as np
from jax import lax
from jax.experimental import pallas as pl
from jax.experimental.pallas import tpu as pltpu

D_MODEL = 1024
CONV_KERNEL = 31
HEAD_DIM = 128
HEADS_PER_GROUP = 4
GROUP_WIDTH = HEADS_PER_GROUP * HEAD_DIM
DILATED_GROUPS = ((128, 1), (512, 4), (2048, 16))
N_GROUPS = len(DILATED_GROUPS)
BLK = 128
ROPE_THETA = 10000.0
EPS = 1e-6
NEG_INF = -1e30
BIG = 3.0e38
LOG2E = math.log2(math.e)

OFF_A, OFF_B, OFF_GC = 0, 1024, 2048
OFF_Q, OFF_K, OFF_V = 3072, 4608, 6144
OFF_GA, OFF_MC, OFF_MA = 7680, 8192, 9216

LANES = 128
SUB_ROWS = 16
CHUNKS_PER_BLK = BLK // SUB_ROWS
PERM_ROWS = 256
GATHER_STRIDE = 4
HALO = 32
CONV_CHUNK = 8
SIDE_PIECE = 256
SIDE_LAG = 8
VMEM_LIMIT = 56 * 1024 * 1024

TM_QKV = 512
TM_TAIL = 512
PROJ_AHEAD = 2
ATTN_PAIRS = 32
PAIRS_PER_BODY = 4


def _params(n_axes):
    return pltpu.CompilerParams(dimension_semantics=("arbitrary",) * n_axes,
                                vmem_limit_bytes=VMEM_LIMIT)


def _const_spec(shape):
    nd = len(shape)
    return pl.BlockSpec(shape, lambda *_: (0,) * nd, pipeline_mode=pl.Buffered(1))


def _sigmoid(v):
    return jax.nn.sigmoid(v)


def _silu(v):
    return v * jax.nn.sigmoid(v)


def _bdot(a, b):
    return jnp.dot(a, b, preferred_element_type=jnp.float32)


def _zero_from(v):
    bits = pltpu.bitcast(v, jnp.uint32)
    return pltpu.bitcast((bits >> 16) >> 16, jnp.float32)


def _mod_kernel(c_ref, w_ref, b_ref, o_ref):
    sc = _silu(c_ref[...])
    o_ref[...] = jnp.dot(sc, w_ref[...], preferred_element_type=jnp.float32,
                         precision=lax.Precision.HIGHEST) + b_ref[...]


def _mod_call(c, w_ada, b_ada):
    bsz = c.shape[0]
    n_out = w_ada.shape[1]
    bn = 1024
    return pl.pallas_call(
        _mod_kernel,
        grid=(n_out // bn,),
        in_specs=[pl.BlockSpec((bsz, D_MODEL), lambda j: (0, 0)),
                  pl.BlockSpec((D_MODEL, bn), lambda j: (0, j)),
                  pl.BlockSpec((1, bn), lambda j: (0, j))],
        out_specs=pl.BlockSpec((bsz, bn), lambda j: (0, j)),
        out_shape=jax.ShapeDtypeStruct((bsz, n_out), jnp.float32),
        compiler_params=_params(1),
        name="mod",
    )(c, w_ada, b_ada.reshape(1, n_out))


def _normed_input(x, norm_w, scale, shift):
    ms = jnp.mean(x * x, axis=-1, keepdims=True)
    return x * lax.rsqrt(ms + EPS) * norm_w * (1.0 + scale) + shift


def _grouped_shape(bsz, seq, dilation, width):
    return (bsz, seq // (SUB_ROWS * dilation), dilation, SUB_ROWS, width)


def _grouped_tile_spec(tm, dilation, width):
    return pl.BlockSpec((1, tm // (SUB_ROWS * dilation), dilation, SUB_ROWS, width),
                        lambda b, s: (b, s, 0, 0, 0))


def _residue_major(val, dilation, buf_a, buf_b):
    tm = val.shape[0]
    if dilation == 1:
        return val
    buf_a[...] = val
    if dilation == GATHER_STRIDE:
        chunk = SUB_ROWS * dilation
        return jnp.concatenate(
            [buf_a[pl.ds(a * chunk + r, SUB_ROWS, stride=dilation), :]
             for a in range(tm // chunk) for r in range(dilation)], axis=0)
    assert dilation == GATHER_STRIDE * GATHER_STRIDE and PERM_ROWS == SUB_ROWS * dilation
    quarter = PERM_ROWS // GATHER_STRIDE
    for blk in range(tm // PERM_ROWS):
        for r0 in range(GATHER_STRIDE):
            buf_b[blk * PERM_ROWS + r0 * quarter:blk * PERM_ROWS + (r0 + 1) * quarter, :] = (
                buf_a[pl.ds(blk * PERM_ROWS + r0, quarter, stride=GATHER_STRIDE), :])
    return jnp.concatenate(
        [buf_b[pl.ds(blk * PERM_ROWS + (r % GATHER_STRIDE) * quarter + r // GATHER_STRIDE, SUB_ROWS,
                     stride=GATHER_STRIDE), :]
         for blk in range(tm // PERM_ROWS) for r in range(dilation)], axis=0)


def _token_order(val, dilation, buf_a, buf_b):
    tm = val.shape[0]
    if dilation == GATHER_STRIDE:
        chunk = SUB_ROWS * dilation
        for a in range(tm // chunk):
            for r in range(dilation):
                lo = a * chunk + r * SUB_ROWS
                buf_a[pl.ds(a * chunk + r, SUB_ROWS, stride=dilation), :] = val[lo:lo + SUB_ROWS]
        return buf_a[...]
    assert dilation == GATHER_STRIDE * GATHER_STRIDE and PERM_ROWS == SUB_ROWS * dilation
    quarter = PERM_ROWS // GATHER_STRIDE
    for blk in range(tm // PERM_ROWS):
        for r in range(dilation):
            lo = blk * PERM_ROWS + r * SUB_ROWS
            buf_b[pl.ds(blk * PERM_ROWS + (r % GATHER_STRIDE) * quarter + r // GATHER_STRIDE, SUB_ROWS,
                        stride=GATHER_STRIDE), :] = val[lo:lo + SUB_ROWS]
    for blk in range(tm // PERM_ROWS):
        for r0 in range(GATHER_STRIDE):
            lo = blk * PERM_ROWS + r0 * quarter
            buf_a[pl.ds(blk * PERM_ROWS + r0, quarter, stride=GATHER_STRIDE), :] = buf_b[lo:lo + quarter, :]
    return buf_a[...]


def _qkv_kernel(x_ref, shift_ref, scale_ref, nw_ref, wq_ref, wk_ref, wv_ref, qnw_ref, knw_ref, pos_ref,
                freq_ref, *out_and_scratch):
    *out_refs, z_ref, buf_a_ref, buf_b_ref, h_ref = out_and_scratch
    w_refs = (wq_ref, wk_ref, wv_ref)
    x = x_ref[0]
    tm = x.shape[0]
    h_ref[...] = _normed_input(x, nw_ref[...], scale_ref[0], shift_ref[0]).astype(jnp.bfloat16)
    half_rows = tm // 2
    half = HEAD_DIM // 2
    pos = pos_ref[0]
    lane = lax.broadcasted_iota(jnp.int32, (half_rows, HEAD_DIM), 1)
    low = lane < half
    ang = jnp.where(low, pos[:half_rows], pos[half_rows:]) * freq_ref[...]
    cos_p, sin_p = jnp.cos(ang), jnp.sin(ang)
    cos_s, sin_s = pltpu.roll(cos_p, half, axis=1), pltpu.roll(sin_p, half, axis=1)
    cos_t = jnp.concatenate([jnp.where(low, cos_p, cos_s), jnp.where(low, cos_s, cos_p)], axis=0)
    sin_t = jnp.concatenate([jnp.where(low, -sin_p, sin_s), jnp.where(low, -sin_s, sin_p)], axis=0)
    norm_ws = (qnw_ref[...] * (LOG2E * HEAD_DIM ** -0.5), knw_ref[...])
    n_proj = 3 * N_GROUPS

    def project(idx):
        kind, g = divmod(idx, N_GROUPS)
        z_ref[idx] = _bdot(h_ref[...], w_refs[kind][:, g * GROUP_WIDTH:(g + 1) * GROUP_WIDTH])

    for idx in range(PROJ_AHEAD):
        project(idx)
    for idx in range(n_proj):
        kind, g = divmod(idx, N_GROUPS)
        dilation = DILATED_GROUPS[g][1]
        heads = []
        for j in range(HEADS_PER_GROUP):
            zh = z_ref[idx, :, j * HEAD_DIM:(j + 1) * HEAD_DIM]
            if kind < 2:
                ms = jnp.mean(zh * zh, axis=-1, keepdims=True)
                zn = zh * lax.rsqrt(ms + EPS) * norm_ws[kind]
                zh = zn * cos_t + pltpu.roll(zn, half, axis=1) * sin_t
            heads.append(_residue_major(zh, dilation, buf_a_ref.at[j], buf_b_ref.at[j]).astype(jnp.bfloat16))
        o_ref = out_refs[g]
        c0 = kind * GROUP_WIDTH
        o_ref[0, :, :, :, c0:c0 + GROUP_WIDTH] = (
            jnp.concatenate(heads, axis=-1).reshape(o_ref.shape[1:-1] + (GROUP_WIDTH,)))
        if idx + PROJ_AHEAD < n_proj:
            done = pltpu.bitcast(o_ref[0, 0, 0, :, c0:c0 + LANES], jnp.uint32)
            words = pltpu.bitcast(h_ref[0:SUB_ROWS, 0:LANES], jnp.uint32) + ((done >> 16) >> 16)
            h_ref[0:SUB_ROWS, 0:LANES] = pltpu.bitcast(words, jnp.bfloat16)
            project(idx + PROJ_AHEAD)


def _weight_cols_spec(offset, width):
    assert offset % width == 0
    return pl.BlockSpec((D_MODEL, width), lambda *_: (0, offset // width), pipeline_mode=pl.Buffered(1))


def _qkv_call(x, mod3, norm_w, w_in_bf, q_norm_w, k_norm_w, posf, freq2):
    bsz, seq, _ = x.shape
    tm = TM_QKV
    tok = lambda b, s: (b, s, 0)
    dils = [d for _, d in DILATED_GROUPS]
    qkv_width = N_GROUPS * GROUP_WIDTH
    return pl.pallas_call(
        _qkv_kernel,
        grid=(bsz, seq // tm),
        in_specs=[pl.BlockSpec((1, tm, D_MODEL), tok),
                  pl.BlockSpec((1, 1, D_MODEL), lambda b, s: (3 * b, 0, 0)),
                  pl.BlockSpec((1, 1, D_MODEL), lambda b, s: (3 * b + 1, 0, 0)),
                  _const_spec((1, D_MODEL)),
                  _weight_cols_spec(OFF_Q, qkv_width),
                  _weight_cols_spec(OFF_K, qkv_width),
                  _weight_cols_spec(OFF_V, qkv_width),
                  _const_spec((1, HEAD_DIM)),
                  _const_spec((1, HEAD_DIM)),
                  pl.BlockSpec((1, tm, HEAD_DIM), tok),
                  _const_spec((1, HEAD_DIM))],
        out_specs=[_grouped_tile_spec(tm, d, qkv_width) for d in dils],
        out_shape=[jax.ShapeDtypeStruct(_grouped_shape(bsz, seq, d, qkv_width), jnp.bfloat16)
                   for d in dils],
        scratch_shapes=[pltpu.VMEM((3 * N_GROUPS, tm, GROUP_WIDTH), jnp.float32),
                        pltpu.VMEM((HEADS_PER_GROUP, tm, HEAD_DIM), jnp.float32),
                        pltpu.VMEM((HEADS_PER_GROUP, tm, HEAD_DIM), jnp.float32),
                        pltpu.VMEM((tm, D_MODEL), jnp.bfloat16)],
        compiler_params=_params(2),
        name="qkv",
    )(x, mod3, mod3, norm_w.reshape(1, D_MODEL), w_in_bf, w_in_bf, w_in_bf,
      q_norm_w.reshape(1, HEAD_DIM), k_norm_w.reshape(1, HEAD_DIM), posf, freq2)


def _attn_kernel(q_ref, k_ref, v_ref, kp_ref, vp_ref, cap_ref, o_ref, l_ref, *, nb_step, res_step):
    no_prev = (pl.program_id(1) == 0).astype(jnp.int32)
    ones = jnp.ones((BLK, HEAD_DIM), jnp.bfloat16)
    lane = lax.broadcasted_iota(jnp.int32, (BLK, HEAD_DIM), 1)
    nt_dims = (((1,), (1,)), ((), ()))

    def tile(ref, chunk0, r, j):
        t = ref[0, pl.ds(chunk0, CHUNKS_PER_BLK), r, :, j * HEAD_DIM:(j + 1) * HEAD_DIM]
        return t.reshape(BLK, HEAD_DIM)

    def process(pairs, first):
        units = [(i, r, j) for i, r in pairs for j in range(HEADS_PER_GROUP)]
        cap = cap_ref[no_prev] if first else cap_ref[0]
        q = [tile(q_ref, i * CHUNKS_PER_BLK, r, j) for i, r, j in units]
        kc = [tile(k_ref, i * CHUNKS_PER_BLK, r, j) for i, r, j in units]
        vc = [tile(v_ref, i * CHUNKS_PER_BLK, r, j) for i, r, j in units]
        if first:
            kp = [tile(kp_ref, 0, r, j) for _, r, j in units]
            vp = [tile(vp_ref, 0, r, j) for _, r, j in units]
        else:
            kp = [tile(k_ref, (i - 1) * CHUNKS_PER_BLK, r, j) for i, r, j in units]
            vp = [tile(v_ref, (i - 1) * CHUNKS_PER_BLK, r, j) for i, r, j in units]
        s = [lax.dot_general(a, jnp.concatenate([b, c], axis=0), nt_dims, preferred_element_type=jnp.float32)
             for a, b, c in zip(q, kp, kc)]
        s = [jnp.minimum(t, cap) for t in s]
        m = [jnp.max(t, axis=-1, keepdims=True) for t in s]
        p = [jnp.exp2(t - mm).astype(jnp.bfloat16) for t, mm in zip(s, m)]
        oe = [_bdot(t, jnp.concatenate([jnp.concatenate([va, ones], axis=-1),
                                        jnp.concatenate([vb, ones], axis=-1)], axis=0))
              for t, va, vb in zip(p, vp, vc)]
        for n, (i, r) in enumerate(pairs):
            chunks = pl.ds(i * CHUNKS_PER_BLK, CHUNKS_PER_BLK)
            stats = jnp.zeros((BLK, HEAD_DIM), jnp.float32)
            for j in range(HEADS_PER_GROUP):
                u = n * HEADS_PER_GROUP + j
                o_ref[0, chunks, r, :, j * HEAD_DIM:(j + 1) * HEAD_DIM] = (
                    oe[u][:, :HEAD_DIM].astype(o_ref.dtype).reshape(CHUNKS_PER_BLK, SUB_ROWS, HEAD_DIM))
                stats = jnp.where(lane == j, m[u], stats)
                stats = jnp.where(lane == HEADS_PER_GROUP + j, oe[u][:, HEAD_DIM:], stats)
            l_ref[0, chunks, r, :, :] = stats.reshape(CHUNKS_PER_BLK, SUB_ROWS, HEAD_DIM)

    def run(n_pairs, pair_of, first):
        left = n_pairs % PAIRS_PER_BODY
        if left:
            process([pair_of(t) for t in range(left)], first)
        if n_pairs >= PAIRS_PER_BODY:
            def body(t, carry):
                t0 = left + PAIRS_PER_BODY * t
                process([pair_of(t0 + n) for n in range(PAIRS_PER_BODY)], first)
                return carry
            lax.fori_loop(0, n_pairs // PAIRS_PER_BODY, body, 0)

    shift = res_step.bit_length() - 1

    def later_pair(t):
        return 1 + (t >> shift), t & (res_step - 1)

    run(res_step, lambda t: (0, t), True)
    run((nb_step - 1) * res_step, later_pair, False)


def _attn_call(qkv, caps, dilation):
    bsz, n_chunks = qkv.shape[:2]
    nb = n_chunks // CHUNKS_PER_BLK
    nb_step = min(nb, ATTN_PAIRS)
    res_step = min(ATTN_PAIRS // nb_step, dilation)
    assert res_step & (res_step - 1) == 0 and nb % nb_step == 0 and dilation % res_step == 0

    def cur(width, col=0):
        return pl.BlockSpec((1, nb_step * CHUNKS_PER_BLK, res_step, SUB_ROWS, width),
                            lambda b, n, rc: (b, n, rc, 0, col))

    def prev(col):
        return pl.BlockSpec((1, CHUNKS_PER_BLK, res_step, SUB_ROWS, GROUP_WIDTH),
                            lambda b, n, rc: (b, jnp.maximum(n * nb_step - 1, 0), rc, 0, col))

    k_prev, v_prev, prev_specs = qkv, qkv, [prev(1), prev(2)]
    if nb_step == nb:
        k_prev = v_prev = jnp.zeros((1, CHUNKS_PER_BLK, dilation, SUB_ROWS, GROUP_WIDTH), qkv.dtype)
        prev_specs = [pl.BlockSpec((1, CHUNKS_PER_BLK, res_step, SUB_ROWS, GROUP_WIDTH),
                                   lambda b, n, rc: (0, 0, rc, 0, 0))] * 2

    group_shape = qkv.shape[:-1]
    return pl.pallas_call(
        functools.partial(_attn_kernel, nb_step=nb_step, res_step=res_step),
        grid=(bsz, nb // nb_step, dilation // res_step),
        in_specs=[cur(GROUP_WIDTH, 0), cur(GROUP_WIDTH, 1), cur(GROUP_WIDTH, 2), *prev_specs,
                  _const_spec(caps.shape)],
        out_specs=[cur(GROUP_WIDTH), cur(HEAD_DIM)],
        out_shape=[jax.ShapeDtypeStruct(group_shape + (GROUP_WIDTH,), jnp.bfloat16),
                   jax.ShapeDtypeStruct(group_shape + (HEAD_DIM,), jnp.float32)],
        compiler_params=_params(3),
        name=f"attn_d{dilation}",
    )(qkv, qkv, qkv, k_prev, v_prev, caps)


def _tail_kernel(x_ref, shift_ref, scale_ref, gate_ref, nw_ref,
                 o0_ref, o1_ref, o2_ref, l0_ref, l1_ref, l2_ref,
                 wa_ref, wb_ref, wgc_ref, wga_ref, wmc_ref, wma_ref,
                 cw_ref, cb_ref, lnw_ref, lnb_ref, wco_ref, wao_ref, wout_ref,
                 out_ref, useq_ref, cseq_ref, side_ref, tok_a_ref, tok_b_ref, tc_ref):
    tm = x_ref.shape[1]
    nsl = D_MODEL // LANES

    @pl.when(pl.program_id(1) == 0)
    def _():
        useq_ref[0:HALO * nsl, :] = jnp.zeros((HALO * nsl, LANES), jnp.float32)

    x = x_ref[0]
    h = _normed_input(x, nw_ref[...], scale_ref[0], shift_ref[0]).astype(jnp.bfloat16)

    for c0 in range(0, D_MODEL, SIDE_PIECE):
        u = _bdot(h, wa_ref[:, c0:c0 + SIDE_PIECE]) * _sigmoid(_bdot(h, wb_ref[:, c0:c0 + SIDE_PIECE]))
        for s in range(SIDE_PIECE // LANES):
            row0 = HALO * nsl + c0 // LANES + s
            useq_ref[pl.ds(row0, tm, stride=nsl), :] = u[:, s * LANES:(s + 1) * LANES]
    def side_piece(w_ref, c0, dst):
        side_ref[:, dst:dst + SIDE_PIECE] = _bdot(h, w_ref[:, c0:c0 + SIDE_PIECE])
        return lambda: side_ref[0:nsl, dst:dst + LANES]

    early_jobs, late_jobs, col = [], [], 0
    for w_ref, early in ((wgc_ref, True), (wga_ref, True), (wmc_ref, False), (wma_ref, False)):
        for c0 in range(0, w_ref.shape[1], SIDE_PIECE):
            (early_jobs if early else late_jobs).append(
                functools.partial(side_piece, w_ref, c0, col + c0))
        col += w_ref.shape[1]
    n_chunks = tm // CONV_CHUNK
    parked = {}
    for c in range(n_chunks):
        base = c * CONV_CHUNK * nsl
        bias = cb_ref[...]
        for read_back in parked.get(c - SIDE_LAG, ()):
            bias = bias + _zero_from(read_back())
        acc = jnp.broadcast_to(bias[None], (CONV_CHUNK, nsl, LANES))
        for t in range(CONV_KERNEL):
            start = base + (HALO - (CONV_KERNEL - 1) + t) * nsl
            taps = useq_ref[pl.ds(start, CONV_CHUNK * nsl), :].reshape(CONV_CHUNK, nsl, LANES)
            acc = acc + cw_ref[t][None] * taps
        cseq_ref[pl.ds(base, CONV_CHUNK * nsl), :] = acc.reshape(CONV_CHUNK * nsl, LANES)
        lo, hi = len(early_jobs) * c // n_chunks, len(early_jobs) * (c + 1) // n_chunks
        parked[c] = [job() for job in early_jobs[lo:hi]]
    late_reads = [job() for job in late_jobs]
    z_gc = side_ref[:, 0:D_MODEL]
    z_ga = side_ref[:, D_MODEL:D_MODEL + GROUP_WIDTH]
    z_mc = side_ref[:, D_MODEL + GROUP_WIDTH:2 * D_MODEL + GROUP_WIDTH]
    z_ma = side_ref[:, 2 * D_MODEL + GROUP_WIDTH:3 * D_MODEL + GROUP_WIDTH]
    useq_ref[0:HALO * nsl, :] = useq_ref[tm * nsl:(tm + HALO) * nsl, :]
    acc = jnp.concatenate([cseq_ref[pl.ds(s, tm, stride=nsl), :] for s in range(nsl)], axis=-1)
    mu = jnp.mean(acc, axis=-1, keepdims=True)
    cen = acc - mu
    var = jnp.mean(cen * cen, axis=-1, keepdims=True)
    ln = cen * lax.rsqrt(var + EPS) * lnw_ref[...] + lnb_ref[...]
    tc_ref[...] = (_silu(ln) * _silu(z_gc)).astype(jnp.bfloat16)
    words = pltpu.bitcast(tc_ref[0:SUB_ROWS, 0:LANES], jnp.uint32)
    for read_back in late_reads:
        words = words + ((pltpu.bitcast(read_back(), jnp.uint32) >> 16) >> 16)
    tc_ref[0:SUB_ROWS, 0:LANES] = pltpu.bitcast(words, jnp.bfloat16)
    y_conv = _bdot(tc_ref[...], wco_ref[...])

    o_g, stats, slot = [], [], 0
    for g, (o_ref, l_ref) in enumerate(((o0_ref, l0_ref), (o1_ref, l1_ref), (o2_ref, l2_ref))):
        dilation = DILATED_GROUPS[g][1]
        o_rows = o_ref[0].reshape(tm, GROUP_WIDTH).astype(jnp.float32)
        slabs = [o_rows[:, j * HEAD_DIM:(j + 1) * HEAD_DIM] for j in range(HEADS_PER_GROUP)]
        slabs.append(l_ref[0].reshape(tm, HEAD_DIM))
        if dilation > 1:
            for n, slab in enumerate(slabs):
                slabs[n] = _token_order(slab, dilation, tok_a_ref.at[slot], tok_b_ref.at[slot])
                slot += 1
        o_g.append(jnp.concatenate(slabs[:HEADS_PER_GROUP], axis=-1))
        stats.append(slabs[HEADS_PER_GROUP])
    top = jnp.maximum(jnp.maximum(stats[0], stats[1]), stats[2])
    e_g = [jnp.exp2(st - top) for st in stats]
    dens = [pltpu.roll(st, LANES - HEADS_PER_GROUP, axis=1) for st in stats]
    total = e_g[0] * dens[0] + e_g[1] * dens[1] + e_g[2] * dens[2]
    head_lane = lax.broadcasted_iota(jnp.int32, total.shape, 1) < HEADS_PER_GROUP
    inv = 1.0 / jnp.where(head_lane, total, 1.0)
    w_g = [e * inv for e in e_g]
    heads = []
    for j in range(HEADS_PER_GROUP):
        cols = slice(j * HEAD_DIM, (j + 1) * HEAD_DIM)
        heads.append(sum(w[:, j:j + 1] * o[:, cols] for w, o in zip(w_g, o_g)))
    o = jnp.concatenate(heads, axis=-1)
    ta = (o * _silu(z_ga)).astype(jnp.bfloat16)
    y_attn = _bdot(ta, wao_ref[...])

    y = _sigmoid(z_mc) * y_conv + _sigmoid(z_ma) * y_attn
    out = _bdot(y.astype(jnp.bfloat16), wout_ref[...])
    out_ref[0] = x + gate_ref[0] * out


def _tail_call(x, mod3, norm_w, o_groups, l_groups, w_in_bf, conv_w, conv_b, ln_w, ln_b,
               w_co, w_ao, w_out):
    side_cols = ((OFF_A, D_MODEL), (OFF_B, D_MODEL), (OFF_GC, D_MODEL),
                 (OFF_GA, GROUP_WIDTH), (OFF_MC, D_MODEL), (OFF_MA, D_MODEL))
    bsz, seq, _ = x.shape
    tm = TM_TAIL
    nsl = D_MODEL // LANES
    tok = lambda b, s: (b, s, 0)
    row = lambda v: v.reshape(1, D_MODEL)
    modspec = lambda k: pl.BlockSpec((1, 1, D_MODEL), lambda b, s: (3 * b + k, 0, 0))
    dils = [d for _, d in DILATED_GROUPS]
    n_relayout = (HEADS_PER_GROUP + 1) * sum(d > 1 for d in dils)
    in_specs = ([pl.BlockSpec((1, tm, D_MODEL), tok), modspec(0), modspec(1), modspec(2),
                 _const_spec((1, D_MODEL))]
                + [_grouped_tile_spec(tm, d, GROUP_WIDTH) for d in dils]
                + [_grouped_tile_spec(tm, d, HEAD_DIM) for d in dils]
                + [_weight_cols_spec(off, width) for off, width in side_cols]
                + [_const_spec((CONV_KERNEL, nsl, LANES)), _const_spec((nsl, LANES))]
                + [_const_spec((1, D_MODEL))] * 2
                + [_const_spec(w_co.shape), _const_spec(w_ao.shape), _const_spec(w_out.shape)])
    return pl.pallas_call(
        _tail_kernel,
        grid=(bsz, seq // tm),
        in_specs=in_specs,
        out_specs=pl.BlockSpec((1, tm, D_MODEL), tok),
        out_shape=jax.ShapeDtypeStruct(x.shape, x.dtype),
        scratch_shapes=[pltpu.VMEM(((tm + HALO) * nsl, LANES), jnp.float32),
                        pltpu.VMEM((tm * nsl, LANES), jnp.float32),
                        pltpu.VMEM((tm, 3 * D_MODEL + GROUP_WIDTH), jnp.float32),
                        pltpu.VMEM((n_relayout, tm, LANES), jnp.float32),
                        pltpu.VMEM((n_relayout, tm, LANES), jnp.float32),
                        pltpu.VMEM((tm, D_MODEL), jnp.bfloat16)],
        compiler_params=_params(2),
        name="tail",
    )(x, mod3, mod3, mod3, row(norm_w), *o_groups, *l_groups, *([w_in_bf] * len(side_cols)),
      conv_w.reshape(CONV_KERNEL, nsl, LANES), conv_b.reshape(nsl, LANES), row(ln_w), row(ln_b),
      w_co, w_ao, w_out)


def _attention_caps():
    qi = np.arange(BLK)[:, None]
    kj = np.arange(BLK)[None, :]
    prev = np.where(kj >= qi, BIG, NEG_INF)
    cur = np.where(kj <= qi, BIG, NEG_INF)
    no_prev = np.full((BLK, BLK), NEG_INF)
    caps = np.stack([np.concatenate([prev, cur], axis=1),
                     np.concatenate([no_prev, cur], axis=1)]).astype(np.float32)
    return jnp.asarray(caps)


def kernel(x, c, positions, norm_w, w_ada, b_ada, w_in, conv_w, conv_b, conv_ln_w, conv_ln_b,
           w_conv_out, q_norm_w, k_norm_w, w_attn_out, w_out):
    bsz, seq, _ = x.shape
    bf = jnp.bfloat16
    for window, dilation in DILATED_GROUPS:
        assert window // dilation == BLK and seq % (dilation * BLK) == 0
        assert PERM_ROWS % (SUB_ROWS * dilation) == 0
    assert seq % TM_QKV == 0 and seq % TM_TAIL == 0
    assert TM_QKV % PERM_ROWS == 0 and TM_TAIL % PERM_ROWS == 0

    mod3 = _mod_call(c, w_ada, b_ada).reshape(bsz * 3, 1, D_MODEL)

    w_in_bf = w_in.astype(bf)
    inv_freq = ROPE_THETA ** (-jnp.arange(0, HEAD_DIM, 2, dtype=jnp.float32) / HEAD_DIM)
    freq2 = jnp.concatenate([inv_freq, inv_freq]).reshape(1, HEAD_DIM)
    posf = jnp.broadcast_to(positions.astype(jnp.float32)[..., None], (bsz, seq, HEAD_DIM))
    qkv = _qkv_call(x, mod3, norm_w, w_in_bf, q_norm_w, k_norm_w, posf, freq2)

    caps = _attention_caps()
    o_groups, l_groups = [], []
    for g, (_, dilation) in enumerate(DILATED_GROUPS):
        o_g, l_g = _attn_call(qkv[g], caps, dilation)
        o_groups.append(o_g)
        l_groups.append(l_g)

    return _tail_call(x, mod3, norm_w, o_groups, l_groups, w_in_bf,
                      conv_w, conv_b, conv_ln_w, conv_ln_b,
                      w_conv_out.astype(bf), w_attn_out.astype(bf), w_out.astype(bf))
```
